```python
import math
import jax, jax.numpy as jnp
from jax import lax
import numpy as np

D_MODEL = 1024
BATCH = 32
SEQ = 2048
DEPTH = 2

HEAD_DIM = 64
N_HEADS = D_MODEL // HEAD_DIM
N_KV_HEADS = N_HEADS // 4
QKV_WIDTH = (N_HEADS + 2 * N_KV_HEADS) * HEAD_DIM
ATTN_HALF_WINDOW = 128
DILATED_GROUPS = ((128, 1), (512, 4), (2048, 16))
N_DGROUPS = len(DILATED_GROUPS)
N_MIXERS = 2
N_LAYERS_A = (DEPTH + 1) // 2
N_LAYERS_B = DEPTH // 2
D_FF = -(-8 * D_MODEL // (3 * 256)) * 256
ROPE_THETA = 10000.0
RMS_EPS = 1e-6
NEG_INF = -1e30

kernel_name = "hybrid_window_sink_dilated_encoder"


def rmsnorm(x, g):
    x32 = x.astype(jnp.float32)
    y = x32 * lax.rsqrt(jnp.mean(x32 * x32, axis=-1, keepdims=True) + RMS_EPS)
    return (y * g.astype(jnp.float32)).astype(x.dtype)


def rope_tables(seq):
    inv_freq = 1.0 / (ROPE_THETA ** (jnp.arange(0, HEAD_DIM, 2, dtype=jnp.float32) / HEAD_DIM))
    ang = jnp.arange(seq, dtype=jnp.float32)[:, None] * inv_freq[None, :]
    return jnp.cos(ang)[:, None, :], jnp.sin(ang)[:, None, :]


def apply_rope(t, cos, sin):
    t32 = t.astype(jnp.float32)
    t1, t2 = jnp.split(t32, 2, axis=-1)
    out = jnp.concatenate([t1 * cos - t2 * sin, t2 * cos + t1 * sin], axis=-1)
    return out.astype(t.dtype)


def split_qkv(proj, cos, sin):
    b, s, _ = proj.shape
    qw = N_HEADS * HEAD_DIM
    kw = N_KV_HEADS * HEAD_DIM
    q = proj[..., :qw].reshape(b, s, N_HEADS, HEAD_DIM)
    k = proj[..., qw:qw + kw].reshape(b, s, N_KV_HEADS, HEAD_DIM)
    v = proj[..., qw + kw:].reshape(b, s, N_KV_HEADS, HEAD_DIM)
    return apply_rope(q, cos, sin), apply_rope(k, cos, sin), v


def banded_attention(q, k, v, half_window, sink=None):
    n, length, n_q, dh = q.shape
    n_kv = k.shape[2]
    grp = n_q // n_kv
    w = half_window
    nb = -(-length // w)
    lp = nb * w
    qb = jnp.pad(q, ((0, 0), (0, lp - length), (0, 0), (0, 0))).reshape(n, nb, w, n_kv, grp, dh)
    pad_kv = ((0, 0), (w, w + lp - length), (0, 0), (0, 0))
    kp = jnp.pad(k, pad_kv)
    vp = jnp.pad(v, pad_kv)
    scale = 1.0 / math.sqrt(dh)
    offs_q = jnp.arange(w)
    offs_k = jnp.arange(3 * w) - w
    sink_l = None if sink is None else sink.astype(jnp.float32).reshape(n_kv, grp)[None, :, :, None]

    def one_block(i):
        start = i * w
        q_i = lax.dynamic_index_in_dim(qb, i, axis=1, keepdims=False)
        k_i = lax.dynamic_slice_in_dim(kp, start, 3 * w, axis=1)
        v_i = lax.dynamic_slice_in_dim(vp, start, 3 * w, axis=1)
        s = jnp.einsum('nqkgd,nskd->nkgqs', q_i, k_i).astype(jnp.float32) * scale
        qpos = start + offs_q
        kpos = start + offs_k
        valid = ((jnp.abs(qpos[:, None] - kpos[None, :]) <= w)
                 & (kpos[None, :] >= 0) & (kpos[None, :] < length))
        s = jnp.where(valid, s, NEG_INF)
        m = jnp.max(s, axis=-1)
        if sink_l is not None:
            m = jnp.maximum(m, sink_l)
        p = jnp.exp(s - m[..., None])
        denom = jnp.sum(p, axis=-1)
        if sink_l is not None:
            denom = denom + jnp.exp(sink_l - m)
        o = jnp.einsum('nkgqs,nskd->nqkgd', p, v_i.astype(jnp.float32))
        o = o / jnp.transpose(denom, (0, 3, 1, 2))[..., None]
        lse = jnp.transpose(m + jnp.log(denom), (0, 3, 1, 2))
        return o.astype(q.dtype), lse

    o, lse = lax.map(one_block, jnp.arange(nb))
    o = jnp.moveaxis(o, 0, 1).reshape(n, lp, n_q, dh)[:, :length]
    lse = jnp.moveaxis(lse, 0, 1).reshape(n, lp, n_q)[:, :length]
    return o, lse


def dilated_attention(q, k, v, dilation, half_window):
    b, s, n_q, dh = q.shape
    d = dilation

    def to_residue(t):
        return t.reshape(b, s // d, d, t.shape[2], dh).transpose(0, 2, 1, 3, 4).reshape(b * d, s // d, t.shape[2], dh)

    o, lse = banded_attention(to_residue(q), to_residue(k), to_residue(v), half_window // d)
    o = o.reshape(b, d, s // d, n_q, dh).transpose(0, 2, 1, 3, 4).reshape(b, s, n_q, dh)
    lse = lse.reshape(b, d, s // d, n_q).transpose(0, 2, 1, 3).reshape(b, s, n_q)
    return o, lse


def mixer_window_sink(h, w_in, sink, w_out, cos, sin):
    b, s, _ = h.shape
    q, k, v = split_qkv(h @ w_in, cos, sin)
    o, _ = banded_attention(q, k, v, ATTN_HALF_WINDOW, sink)
    return o.reshape(b, s, N_HEADS * HEAD_DIM) @ w_out


def mixer_dilated(h, w_in, w_out, cos, sin):
    b, s, _ = h.shape
    proj = (h @ w_in).reshape(b, s, N_DGROUPS, QKV_WIDTH)
    outs, lses = [], []
    for g, (window, dilation) in enumerate(DILATED_GROUPS):
        q, k, v = split_qkv(proj[:, :, g], cos, sin)
        o, lse = dilated_attention(q, k, v, dilation, window // 2)
        outs.append(o)
        lses.append(lse)
    wts = jax.nn.softmax(jnp.stack(lses, axis=0), axis=0)
    o = (wts[0][..., None] * outs[0].astype(jnp.float32)
         + wts[1][..., None] * outs[1].astype(jnp.float32)
         + wts[2][..., None] * outs[2].astype(jnp.float32))
    return o.astype(h.dtype).reshape(b, s, N_HEADS * HEAD_DIM) @ w_out


def swiglu(h, w_gate, w_up, w_down):
    return (jax.nn.silu(h @ w_gate) * (h @ w_up)) @ w_down


def setup_inputs(seed: int = 0) -> dict:
    key = jax.random.key(seed)
    ks = jax.random.split(key, 14)
    f32 = jnp.float32
    d = D_MODEL
    hd = N_HEADS * HEAD_DIM
    x = jax.random.normal(ks[0], (BATCH, SEQ, d), f32)
    a_w_in = jax.random.normal(ks[1], (N_LAYERS_A, d, QKV_WIDTH), f32) * d ** -0.5
    a_sink = jax.random.normal(ks[2], (N_LAYERS_A, N_HEADS), f32) * 0.5
    a_w_out = jax.random.normal(ks[3], (N_LAYERS_A, hd, d), f32) * hd ** -0.5
    b_w_in = jax.random.normal(ks[4], (N_LAYERS_B, d, N_DGROUPS * QKV_WIDTH), f32) * d ** -0.5
    b_w_out = jax.random.normal(ks[5], (N_LAYERS_B, hd, d), f32) * hd ** -0.5
    norm_mix = 1.0 + 0.02 * jax.random.normal(ks[6], (DEPTH, d), f32)
    norm_ffn = 1.0 + 0.02 * jax.random.normal(ks[7], (DEPTH, d), f32)
    w_gate = jax.random.normal(ks[8], (DEPTH, d, D_FF), f32) * d ** -0.5
    w_up = jax.random.normal(ks[9], (DEPTH, d, D_FF), f32) * d ** -0.5
    w_down = jax.random.normal(ks[10], (DEPTH, D_FF, d), f32) * D_FF ** -0.5
    final_norm = 1.0 + 0.02 * jax.random.normal(ks[11], (d,), f32)
    return {"x": x, "a_w_in": a_w_in, "a_sink": a_sink, "a_w_out": a_w_out,
            "b_w_in": b_w_in, "b_w_out": b_w_out, "norm_mix": norm_mix, "norm_ffn": norm_ffn,
            "w_gate": w_gate, "w_up": w_up, "w_down": w_down, "final_norm": final_norm}


def reference(x, a_w_in, a_sink, a_w_out, b_w_in, b_w_out, norm_mix, norm_ffn,
              w_gate, w_up, w_down, final_norm):
    cos, sin = rope_tables(x.shape[1])
    for i in range(DEPTH):
        h = rmsnorm(x, norm_mix[i])
        j = i // N_MIXERS
        if i % N_MIXERS == 0:
            mix = mixer_window_sink(h, a_w_in[j], a_sink[j], a_w_out[j], cos, sin)
        else:
            mix = mixer_dilated(h, b_w_in[j], b_w_out[j], cos, sin)
        x = x + mix
        h = rmsnorm(x, norm_ffn[i])
        x = x + swiglu(h, w_gate[i], w_up[i], w_down[i])
    return rmsnorm(x, final_norm)
```

```python
import functools
import math

import jax
import jax.numpy as jnp
from jax import lax
from jax.experimental import pallas as pl
from jax.experimental.pallas import tpu as pltpu

D_MODEL = 1024
HEAD_DIM = 64
N_HEADS = 16
N_KV_HEADS = 4
GQA = N_HEADS // N_KV_HEADS
Q_WIDTH = N_HEADS * HEAD_DIM
KV_WIDTH = N_KV_HEADS * HEAD_DIM
QKV_WIDTH = Q_WIDTH + 2 * KV_WIDTH
ATTN_HALF_WINDOW = 128
DILATED_GROUPS = ((128, 1), (512, 4), (2048, 16))
ROPE_THETA = 10000.0
RMS_EPS = 1e-6
NEG_INF = -1e30
LANES = 128
VMEM_LIMIT = 56 * 1024 * 1024

BF16 = jnp.bfloat16
F32 = jnp.float32


def _rmsnorm(x, g):
    ms = jnp.mean(x * x, axis=-1, keepdims=True)
    return x * lax.rsqrt(ms + RMS_EPS) * g


def _proj_kernel(x_ref, g_ref, w_ref, cos_ref, sin_ref, o_ref):
    tm = x_ref.shape[0]
    h = _rmsnorm(x_ref[...], g_ref[...]).astype(BF16)
    y = jnp.dot(h, w_ref[...], preferred_element_type=F32)
    cos = cos_ref[...]
    sin = sin_ref[...]
    lane = lax.broadcasted_iota(jnp.int32, (tm, LANES), 1)
    first_half = (lane % HEAD_DIM) < (HEAD_DIM // 2)
    n_rope = (Q_WIDTH + KV_WIDTH) // LANES
    n_q = Q_WIDTH // LANES
    for c in range(QKV_WIDTH // LANES):
        t = y[:, c * LANES:(c + 1) * LANES]
        if c < n_rope:
            swapped = jnp.where(first_half,
                                pltpu.roll(t, LANES - HEAD_DIM // 2, 1),
                                pltpu.roll(t, HEAD_DIM // 2, 1))
            t = t * cos + swapped * sin
            if c < n_q:
                t = t * (1.0 / math.sqrt(HEAD_DIM))
        o_ref[:, c * LANES:(c + 1) * LANES] = t.astype(BF16)


def _proj(x2d, g, w, cos_t, sin_t, seq, tm=512):
    t = x2d.shape[0]
    n_seq_blocks = seq // tm
    return pl.pallas_call(
        _proj_kernel,
        grid=(t // tm,),
        in_specs=[
            pl.BlockSpec((tm, D_MODEL), lambda i: (i, 0)),
            pl.BlockSpec((1, D_MODEL), lambda i: (0, 0)),
            pl.BlockSpec((D_MODEL, QKV_WIDTH), lambda i: (0, 0)),
            pl.BlockSpec((tm, LANES), lambda i: (i % n_seq_blocks, 0)),
            pl.BlockSpec((tm, LANES), lambda i: (i % n_seq_blocks, 0)),
        ],
        out_specs=pl.BlockSpec((tm, QKV_WIDTH), lambda i: (i, 0)),
        out_shape=jax.ShapeDtypeStruct((t, QKV_WIDTH), BF16),
        compiler_params=pltpu.CompilerParams(
            dimension_semantics=("arbitrary",), vmem_limit_bytes=VMEM_LIMIT),
        name="qkv_proj",
    )(x2d, g, w, cos_t, sin_t)


def _attn_kernel(*refs, seq, d, w, has_sink):
    if has_sink:
        sink_ref, qkv_ref, o_ref = refs
        lse_ref = None
    else:
        qkv_ref, o_ref, lse_ref = refs
    length = seq // d
    nb = length // w
    nk = min(3 * w, length)

    qi = lax.broadcasted_iota(jnp.int32, (w, nk), 0)
    ki = lax.broadcasted_iota(jnp.int32, (w, nk), 1)
    dif0 = qi - ki
    lane = lax.broadcasted_iota(jnp.int32, (w, LANES), 1)

    def body(t, carry):
        r = t // nb
        i = t % nb
        q0 = i * w
        ks = jnp.clip(q0 - w, 0, length - nk)
        valid = jnp.abs(dif0 + (q0 - ks)) <= w
        valid4 = jnp.concatenate([valid] * GQA, axis=0)
        row0 = pl.multiple_of(r * length + q0, w)
        krow0 = pl.multiple_of(r * length + ks, w)
        lse_tile = jnp.zeros((w, LANES), F32)
        for g in range(N_KV_HEADS):
            q4 = jnp.concatenate(
                [qkv_ref[0, pl.ds(row0, w), (GQA * g + hh) * HEAD_DIM:(GQA * g + hh + 1) * HEAD_DIM]
                 for hh in range(GQA)], axis=0)
            k = qkv_ref[0, pl.ds(krow0, nk), Q_WIDTH + g * HEAD_DIM:Q_WIDTH + (g + 1) * HEAD_DIM]
            v = qkv_ref[0, pl.ds(krow0, nk),
                        Q_WIDTH + KV_WIDTH + g * HEAD_DIM:Q_WIDTH + KV_WIDTH + (g + 1) * HEAD_DIM]
            s = lax.dot_general(q4, k, (((1,), (1,)), ((), ())),
                                preferred_element_type=F32)
            s = jnp.where(valid4, s, NEG_INF)
            m = jnp.max(s, axis=-1, keepdims=True)
            if has_sink:
                sink_col = jnp.concatenate(
                    [jnp.full((w, 1), sink_ref[GQA * g + hh], F32) for hh in range(GQA)], axis=0)
                m = jnp.maximum(m, sink_col)
            p = jnp.exp(s - m)
            l = jnp.sum(p, axis=-1, keepdims=True)
            if has_sink:
                l = l + jnp.exp(sink_col - m)
            o = jnp.dot(p.astype(BF16), v, preferred_element_type=F32)
            o = o / l
            o_cat = jnp.concatenate([o[hh * w:(hh + 1) * w] for hh in range(GQA)], axis=1)
            o_ref[0, pl.ds(row0, w), g * GQA * HEAD_DIM:(g + 1) * GQA * HEAD_DIM] = o_cat.astype(BF16)
            if lse_ref is not None:
                lse4 = m + jnp.log(l)
                for hh in range(GQA):
                    lse_tile = jnp.where(lane == GQA * g + hh, lse4[hh * w:(hh + 1) * w], lse_tile)
        if lse_ref is not None:
            lse_ref[0, pl.ds(row0, w), :] = lse_tile
        return carry

    lax.fori_loop(0, d * nb, body, 0)


def _attn(qkv, d, w, sink=None):
    b, seq, _ = qkv.shape
    has_sink = sink is not None
    kern = functools.partial(_attn_kernel, seq=seq, d=d, w=w, has_sink=has_sink)
    in_specs = [pl.BlockSpec((1, seq, QKV_WIDTH), lambda i: (i, 0, 0))]
    args = [qkv]
    o_spec = pl.BlockSpec((1, seq, Q_WIDTH), lambda i: (i, 0, 0))
    o_shape = jax.ShapeDtypeStruct((b, seq, Q_WIDTH), BF16)
    if has_sink:
        in_specs = [pl.BlockSpec(memory_space=pltpu.SMEM)] + in_specs
        args = [sink] + args
        out_specs, out_shape = o_spec, o_shape
    else:
        out_specs = (o_spec, pl.BlockSpec((1, seq, LANES), lambda i: (i, 0, 0)))
        out_shape = (o_shape, jax.ShapeDtypeStruct((b, seq, LANES), F32))
    return pl.pallas_call(
        kern,
        grid=(b,),
        in_specs=in_specs,
        out_specs=out_specs,
        out_shape=out_shape,
        compiler_params=pltpu.CompilerParams(
            dimension_semantics=("arbitrary",), vmem_limit_bytes=VMEM_LIMIT),
        name=f"band_attn_d{d}",
    )(*args)


def _post_kernel(*refs, n_groups, final, ff_chunk):
    x_ref = refs[0]
    o_refs = refs[1:1 + n_groups]
    pos = 1 + n_groups
    lse_refs = refs[pos:pos + (n_groups if n_groups > 1 else 0)]
    pos += len(lse_refs)
    wo_ref, g_ref, wg_ref, wu_ref, wd_ref = refs[pos:pos + 5]
    pos += 5
    if final:
        gf_ref = refs[pos]
        pos += 1
    out_ref, act_ref = refs[pos:pos + 2]

    tm = x_ref.shape[0]
    if n_groups == 1:
        o = o_refs[0][...]
    else:
        lses = [r[...] for r in lse_refs]
        mx = functools.reduce(jnp.maximum, lses)
        es = [jnp.exp(l - mx) for l in lses]
        inv = 1.0 / functools.reduce(jnp.add, es)
        wts = [e * inv for e in es]
        cols = []
        for hp in range(N_HEADS // 2):
            lane = lax.broadcasted_iota(jnp.int32, (tm, LANES), 1)
            acc = jnp.zeros((tm, LANES), F32)
            for gi in range(n_groups):
                wa = jnp.broadcast_to(wts[gi][:, 2 * hp:2 * hp + 1], (tm, LANES))
                wb = jnp.broadcast_to(wts[gi][:, 2 * hp + 1:2 * hp + 2], (tm, LANES))
                wsel = jnp.where(lane < HEAD_DIM, wa, wb)
                acc = acc + wsel * o_refs[gi][:, hp * LANES:(hp + 1) * LANES].astype(F32)
            cols.append(acc.astype(BF16))
        o = jnp.concatenate(cols, axis=1)
    x1 = x_ref[...] + jnp.dot(o, wo_ref[...], preferred_element_type=F32)
    h = _rmsnorm(x1, g_ref[...]).astype(BF16)
    d_ff = wg_ref.shape[1]
    for c in range(d_ff // ff_chunk):
        sl = slice(c * ff_chunk, (c + 1) * ff_chunk)
        gate = jnp.dot(h, wg_ref[:, sl], preferred_element_type=F32)
        up = jnp.dot(h, wu_ref[:, sl], preferred_element_type=F32)
        act_ref[:, sl] = (gate * jax.nn.sigmoid(gate) * up).astype(BF16)
    x2 = x1 + jnp.dot(act_ref[...], wd_ref[...], preferred_element_type=F32)
    if final:
        x2 = _rmsnorm(x2, gf_ref[...])
    out_ref[...] = x2


def _post(x2d, os_, lses, wo, g, wg, wu, wd, gf=None, tm=512, ff_chunk=704):
    t = x2d.shape[0]
    n_groups = len(os_)
    final = gf is not None
    d_ff = wg.shape[1]
    tile = lambda i: (i, 0)
    const = lambda i: (0, 0)
    resident = functools.partial(pl.BlockSpec, index_map=const, pipeline_mode=pl.Buffered(1))
    in_specs = [pl.BlockSpec((tm, D_MODEL), tile)]
    in_specs += [pl.BlockSpec((tm, Q_WIDTH), tile)] * n_groups
    in_specs += [pl.BlockSpec((tm, LANES), tile)] * len(lses)
    in_specs += [resident((Q_WIDTH, D_MODEL)), resident((1, D_MODEL)),
                 resident((D_MODEL, d_ff)), resident((D_MODEL, d_ff)), resident((d_ff, D_MODEL))]
    args = [x2d, *os_, *lses, wo, g, wg, wu, wd]
    if final:
        in_specs.append(resident((1, D_MODEL)))
        args.append(gf)
    kern = functools.partial(_post_kernel, n_groups=n_groups, final=final, ff_chunk=ff_chunk)
    return pl.pallas_call(
        kern,
        grid=(t // tm,),
        in_specs=in_specs,
        out_specs=pl.BlockSpec((tm, D_MODEL), tile),
        out_shape=jax.ShapeDtypeStruct((t, D_MODEL), F32),
        scratch_shapes=[pltpu.VMEM((tm, d_ff), BF16)],
        compiler_params=pltpu.CompilerParams(
            dimension_semantics=("arbitrary",), vmem_limit_bytes=VMEM_LIMIT),
        name=f"out_ffn_g{n_groups}",
    )(*args)


def _rope_tables(seq):
    inv_freq = 1.0 / (ROPE_THETA ** (jnp.arange(0, HEAD_DIM, 2, dtype=F32) / HEAD_DIM))
    ang = jnp.arange(seq, dtype=F32)[:, None] * inv_freq[None, :]
    cos = jnp.cos(ang)
    sin = jnp.sin(ang)
    cos_t = jnp.tile(cos, (1, LANES // (HEAD_DIM // 2)))
    sin_t = jnp.tile(jnp.concatenate([-sin, sin], axis=1), (1, LANES // HEAD_DIM))
    return cos_t, sin_t


def _to_residue(a, d):
    b, s = a.shape[0], a.shape[1]
    rest = a.shape[2:]
    return a.reshape(b, s // d, d, *rest).swapaxes(1, 2).reshape(b, s, *rest)


def _from_residue(a, d):
    b, s = a.shape[0], a.shape[1]
    rest = a.shape[2:]
    return a.reshape(b, d, s // d, *rest).swapaxes(1, 2).reshape(b, s, *rest)


def kernel(x, a_w_in, a_sink, a_w_out, b_w_in, b_w_out, norm_mix, norm_ffn,
           w_gate, w_up, w_down, final_norm):
    b, s, dm = x.shape
    t = b * s
    cos_t, sin_t = _rope_tables(s)
    row = lambda v: v.reshape(1, -1).astype(F32)
    bf = lambda a: a.astype(BF16)

    x2d = x.reshape(t, dm)
    qkv = _proj(x2d, row(norm_mix[0]), bf(a_w_in[0]), cos_t, sin_t, s)
    o = _attn(qkv.reshape(b, s, QKV_WIDTH), 1, ATTN_HALF_WINDOW, sink=a_sink[0].astype(F32))
    x2d = _post(x2d, [o.reshape(t, Q_WIDTH)], [], bf(a_w_out[0]), row(norm_ffn[0]),
                bf(w_gate[0]), bf(w_up[0]), bf(w_down[0]))

    x3 = x2d.reshape(b, s, dm)
    os_, lses = [], []
    for gi, (window, dil) in enumerate(DILATED_GROUPS):
        w_g = bf(b_w_in[0][:, gi * QKV_WIDTH:(gi + 1) * QKV_WIDTH])
        xg = _to_residue(x3, dil).reshape(t, dm)
        cg = _to_residue(cos_t[None], dil)[0]
        sg = _to_residue(sin_t[None], dil)[0]
        qkv = _proj(xg, row(norm_mix[1]), w_g, cg, sg, s)
        o, lse = _attn(qkv.reshape(b, s, QKV_WIDTH), dil, window // 2 // dil)
        os_.append(_from_residue(o, dil).reshape(t, Q_WIDTH))
        lses.append(_from_residue(lse, dil).reshape(t, LANES))
    out = _post(x2d, os_, lses, bf(b_w_out[0]), row(norm_ffn[1]),
                bf(w_gate[1]), bf(w_up[1]), bf(w_down[1]), gf=row(final_norm))
    return out.reshape(b, s, dm)
```

```python
import functools
import math

import jax
import jax.numpy as jnp
from jax import lax
from jax.experimental import pallas as pl
from jax.experimental.pallas import tpu as pltpu

D_MODEL = 1024
HEAD_DIM = 64
N_HEADS = 16
N_KV_HEADS = 4
GQA = N_HEADS // N_KV_HEADS
Q_WIDTH = N_HEADS * HEAD_DIM
KV_WIDTH = N_KV_HEADS * HEAD_DIM
QKV_WIDTH = Q_WIDTH + 2 * KV_WIDTH
ATTN_HALF_WINDOW = 128
DILATED_GROUPS = ((128, 1), (512, 4), (2048, 16))
ROPE_THETA = 10000.0
RMS_EPS = 1e-6
NEG_INF = -1e30
LANES = 128
VMEM_LIMIT = 56 * 1024 * 1024

BF16 = jnp.bfloat16
F32 = jnp.float32


def _rmsnorm(x, g):
    ms = jnp.mean(x * x, axis=-1, keepdims=True)
    return x * lax.rsqrt(ms + RMS_EPS) * g


def _proj_kernel(x_ref, g_ref, w_ref, cos_ref, sin_ref, o_ref, *, d):
    tm = x_ref.shape[0]
    x = x_ref[...]
    if d > 1:
        x = jnp.swapaxes(x.reshape(tm // d, d, D_MODEL), 0, 1).reshape(tm, D_MODEL)
    h = _rmsnorm(x, g_ref[...]).astype(BF16)
    y = jnp.dot(h, w_ref[...], preferred_element_type=F32)
    cos = cos_ref[...]
    sin = sin_ref[...]
    lane = lax.broadcasted_iota(jnp.int32, (tm, LANES), 1)
    first_half = (lane % HEAD_DIM) < (HEAD_DIM // 2)
    n_rope = (Q_WIDTH + KV_WIDTH) // LANES
    n_q = Q_WIDTH // LANES
    for c in range(QKV_WIDTH // LANES):
        t = y[:, c * LANES:(c + 1) * LANES]
        if c < n_rope:
            swapped = jnp.where(first_half,
                                pltpu.roll(t, LANES - HEAD_DIM // 2, 1),
                                pltpu.roll(t, HEAD_DIM // 2, 1))
            t = t * cos + swapped * sin
            if c < n_q:
                t = t * (1.0 / math.sqrt(HEAD_DIM))
        o_ref[0, :, :, c * LANES:(c + 1) * LANES] = t.astype(BF16).reshape(d, tm // d, LANES)


def _tile_residue_order(table, d, tm):
    s, n = table.shape
    return table.reshape(s // tm, tm // d, d, n).swapaxes(1, 2).reshape(s, n)


def _proj(x2d, g, w, cos_t, sin_t, batch, seq, d, tm=512):
    n_seq_blocks = seq // tm
    if d > 1:
        cos_t = _tile_residue_order(cos_t, d, tm)
        sin_t = _tile_residue_order(sin_t, d, tm)
    out = pl.pallas_call(
        functools.partial(_proj_kernel, d=d),
        grid=(batch * n_seq_blocks,),
        in_specs=[
            pl.BlockSpec((tm, D_MODEL), lambda i: (i, 0)),
            pl.BlockSpec((1, D_MODEL), lambda i: (0, 0)),
            pl.BlockSpec((D_MODEL, QKV_WIDTH), lambda i: (0, 0)),
            pl.BlockSpec((tm, LANES), lambda i: (i % n_seq_blocks, 0)),
            pl.BlockSpec((tm, LANES), lambda i: (i % n_seq_blocks, 0)),
        ],
        out_specs=pl.BlockSpec((1, d, tm // d, QKV_WIDTH),
                               lambda i: (i // n_seq_blocks, 0, i % n_seq_blocks, 0)),
        out_shape=jax.ShapeDtypeStruct((batch, d, seq // d, QKV_WIDTH), BF16),
        compiler_params=pltpu.CompilerParams(
            dimension_semantics=("arbitrary",), vmem_limit_bytes=VMEM_LIMIT),
        name=f"qkv_proj_d{d}",
    )(x2d, g, w, cos_t, sin_t)
    return out.reshape(batch, seq, QKV_WIDTH)


Q_BLOCK = 128


def _attn_kernel(*refs, seq, d, w, has_sink):
    if has_sink:
        sink_ref, qkv_ref, o_ref, s_scr, m_scr = refs
        lse_ref = None
    else:
        qkv_ref, o_ref, lse_ref, s_scr, m_scr = refs
    qb = Q_BLOCK
    length = seq // d
    nqb = length // qb
    nk = min(qb + 2 * w, length)
    n_units = d * nqb
    nl = GQA * qb

    dif0 = (lax.broadcasted_iota(jnp.int32, (nk, qb), 0)
            - lax.broadcasted_iota(jnp.int32, (nk, qb), 1))

    def unit_rows(t):
        r = t // nqb
        q0 = (t % nqb) * qb
        ks = jnp.clip(q0 - w, 0, length - nk)
        valid = jnp.abs(dif0 + (ks - q0)) <= w
        valid4 = jnp.concatenate([valid] * GQA, axis=1)
        row0 = pl.multiple_of(r * length + q0, qb)
        krow0 = pl.multiple_of(r * length + ks, 16)
        return row0, krow0, valid4

    def scores(row0, krow0, valid4, g):
        k = qkv_ref[0, pl.ds(krow0, nk), Q_WIDTH + g * HEAD_DIM:Q_WIDTH + (g + 1) * HEAD_DIM]
        q4 = jnp.concatenate(
            [qkv_ref[0, pl.ds(row0, qb), (GQA * g + hh) * HEAD_DIM:(GQA * g + hh + 1) * HEAD_DIM]
             for hh in range(GQA)], axis=0)
        s = lax.dot_general(k, q4, (((1,), (1,)), ((), ())), preferred_element_type=F32)
        s = jnp.where(valid4, s, NEG_INF)
        m = jnp.max(s, axis=0, keepdims=True)
        if has_sink:
            m = jnp.maximum(m, sink_ref[:, g * nl:(g + 1) * nl])
        return s, m

    def finish(row0, krow0, g, s, m):
        p = jnp.exp(s - m)
        l = jnp.sum(p, axis=0, keepdims=True)
        if has_sink:
            l = l + jnp.exp(sink_ref[:, g * nl:(g + 1) * nl] - m)
        v = qkv_ref[0, pl.ds(krow0, nk),
                    Q_WIDTH + KV_WIDTH + g * HEAD_DIM:Q_WIDTH + KV_WIDTH + (g + 1) * HEAD_DIM]
        o_t = lax.dot_general(v, p.astype(BF16), (((0,), (0,)), ((), ())),
                              preferred_element_type=F32)
        o_t = o_t * (1.0 / l)
        for hp in range(GQA // 2):
            pair = jnp.concatenate([o_t[:, (2 * hp) * qb:(2 * hp + 1) * qb],
                                    o_t[:, (2 * hp + 1) * qb:(2 * hp + 2) * qb]], axis=0)
            c0 = (GQA * g + 2 * hp) * HEAD_DIM
            o_ref[0, pl.ds(row0, qb), c0:c0 + 2 * HEAD_DIM] = pair.T.astype(BF16)
        if lse_ref is None:
            return []
        lse = m + jnp.log(l)
        return [lse[:, hh * qb:(hh + 1) * qb] for hh in range(GQA)]

    row0, krow0, valid4 = unit_rows(0)
    s0, m0 = scores(row0, krow0, valid4, 0)
    s_scr[...] = s0
    m_scr[...] = m0

    def body(t, carry):
        row0, krow0, valid4 = unit_rows(t)
        nrow0, nkrow0, nvalid4 = unit_rows(jnp.minimum(t + 1, n_units - 1))
        cur = (s_scr[...], m_scr[...])
        lse_rows = []
        for g in range(N_KV_HEADS):
            if g + 1 < N_KV_HEADS:
                nxt = scores(row0, krow0, valid4, g + 1)
            else:
                nxt = scores(nrow0, nkrow0, nvalid4, 0)
                s_scr[...] = nxt[0]
                m_scr[...] = nxt[1]
            lse_rows += finish(row0, krow0, g, *cur)
            cur = nxt
        if lse_ref is not None:
            pad = jnp.zeros((LANES - N_HEADS, qb), F32)
            lse_ref[0, pl.ds(row0, qb), :] = jnp.concatenate(lse_rows + [pad], axis=0).T
        return carry

    lax.fori_loop(0, n_units, body, 0)


def _attn(qkv, d, w, sink_rows=None):
    b, seq, _ = qkv.shape
    has_sink = sink_rows is not None
    nk = min(Q_BLOCK + 2 * w, seq // d)
    kern = functools.partial(_attn_kernel, seq=seq, d=d, w=w, has_sink=has_sink)
    in_specs = [pl.BlockSpec((1, seq, QKV_WIDTH), lambda i: (i, 0, 0))]
    args = [qkv]
    o_spec = pl.BlockSpec((1, seq, Q_WIDTH), lambda i: (i, 0, 0))
    o_shape = jax.ShapeDtypeStruct((b, seq, Q_WIDTH), BF16)
    if has_sink:
        in_specs = [pl.BlockSpec((1, N_HEADS * Q_BLOCK), lambda i: (0, 0))] + in_specs
        args = [sink_rows] + args
        out_specs, out_shape = o_spec, o_shape
    else:
        out_specs = (o_spec, pl.BlockSpec((1, seq, LANES), lambda i: (i, 0, 0)))
        out_shape = (o_shape, jax.ShapeDtypeStruct((b, seq, LANES), F32))
    return pl.pallas_call(
        kern,
        grid=(b,),
        in_specs=in_specs,
        out_specs=out_specs,
        out_shape=out_shape,
        scratch_shapes=[pltpu.VMEM((nk, GQA * Q_BLOCK), F32), pltpu.VMEM((1, GQA * Q_BLOCK), F32)],
        compiler_params=pltpu.CompilerParams(
            dimension_semantics=("arbitrary",), vmem_limit_bytes=VMEM_LIMIT),
        name=f"band_attn_d{d}",
    )(*args)


def _mix_kernel(*refs, dils):
    n = len(dils)
    o_refs, lse_refs = refs[:n], refs[n:2 * n]
    e_ref, out_ref = refs[2 * n:]
    tm = out_ref.shape[0]

    def token_order(ref, d):
        a = ref[0].astype(F32)
        if d > 1:
            a = jnp.swapaxes(a, 0, 1)
        return a.reshape(tm, a.shape[-1])

    lses = [token_order(r, d) for r, d in zip(lse_refs, dils)]
    mx = functools.reduce(jnp.maximum, lses)
    es = [jnp.exp(l - mx) for l in lses]
    inv = 1.0 / functools.reduce(jnp.add, es)
    acc = jnp.zeros((tm, Q_WIDTH), F32)
    for e, o_ref, d in zip(es, o_refs, dils):
        wt = e * inv
        hi = wt.astype(BF16)
        lo = (wt - hi.astype(F32)).astype(BF16)
        wfull = jnp.dot(jnp.concatenate([hi, lo], axis=1), e_ref[...], preferred_element_type=F32)
        acc = acc + wfull * token_order(o_ref, d)
    out_ref[...] = acc.astype(BF16)


def _mix(os_, lses, dils, batch, seq, tm=512):
    n_seq_blocks = seq // tm
    head_of_lane = jnp.arange(Q_WIDTH) // HEAD_DIM
    spread = (jnp.arange(LANES)[:, None] == head_of_lane[None, :]).astype(BF16)
    spread = jnp.concatenate([spread, spread], axis=0)
    idx = lambda i: (i // n_seq_blocks, 0, i % n_seq_blocks, 0)
    in_specs, args = [], []
    for a, width in ((os_, Q_WIDTH), (lses, LANES)):
        for arr, d in zip(a, dils):
            in_specs.append(pl.BlockSpec((1, d, tm // d, width), idx))
            args.append(arr.reshape(batch, d, seq // d, width))
    in_specs.append(pl.BlockSpec((2 * LANES, Q_WIDTH), lambda i: (0, 0)))
    return pl.pallas_call(
        functools.partial(_mix_kernel, dils=tuple(dils)),
        grid=(batch * n_seq_blocks,),
        in_specs=in_specs,
        out_specs=pl.BlockSpec((tm, Q_WIDTH), lambda i: (i, 0)),
        out_shape=jax.ShapeDtypeStruct((batch * seq, Q_WIDTH), BF16),
        compiler_params=pltpu.CompilerParams(
            dimension_semantics=("arbitrary",), vmem_limit_bytes=VMEM_LIMIT),
        name="group_mix",
    )(*args, spread)


def _post_kernel(*refs, final, ff_chunk):
    x_ref, o_ref, wo_ref, g_ref, wg_ref, wu_ref, wd_ref = refs[:7]
    gf_ref = refs[7] if final else None
    out_ref, act_ref = refs[-2:]

    x1 = x_ref[...] + jnp.dot(o_ref[...], wo_ref[...], preferred_element_type=F32)
    h = _rmsnorm(x1, g_ref[...]).astype(BF16)
    d_ff = wg_ref.shape[1]
    for c in range(d_ff // ff_chunk):
        sl = slice(c * ff_chunk, (c + 1) * ff_chunk)
        gate = jnp.dot(h, wg_ref[:, sl], preferred_element_type=F32)
        up = jnp.dot(h, wu_ref[:, sl], preferred_element_type=F32)
        act_ref[:, sl] = (gate * jax.nn.sigmoid(gate) * up).astype(BF16)
    x2 = x1 + jnp.dot(act_ref[...], wd_ref[...], preferred_element_type=F32)
    if final:
        x2 = _rmsnorm(x2, gf_ref[...])
    out_ref[...] = x2


def _post(x2d, o2d, wo, g, wg, wu, wd, gf=None, tm=512, ff_chunk=704):
    t = x2d.shape[0]
    final = gf is not None
    d_ff = wg.shape[1]
    tile = lambda i: (i, 0)
    const = lambda i: (0, 0)
    resident = functools.partial(pl.BlockSpec, index_map=const, pipeline_mode=pl.Buffered(1))
    in_specs = [pl.BlockSpec((tm, D_MODEL), tile), pl.BlockSpec((tm, Q_WIDTH), tile),
                resident((Q_WIDTH, D_MODEL)), resident((1, D_MODEL)),
                resident((D_MODEL, d_ff)), resident((D_MODEL, d_ff)), resident((d_ff, D_MODEL))]
    args = [x2d, o2d, wo, g, wg, wu, wd]
    if final:
        in_specs.append(resident((1, D_MODEL)))
        args.append(gf)
    return pl.pallas_call(
        functools.partial(_post_kernel, final=final, ff_chunk=ff_chunk),
        grid=(t // tm,),
        in_specs=in_specs,
        out_specs=pl.BlockSpec((tm, D_MODEL), tile),
        out_shape=jax.ShapeDtypeStruct((t, D_MODEL), F32),
        scratch_shapes=[pltpu.VMEM((tm, d_ff), BF16)],
        compiler_params=pltpu.CompilerParams(
            dimension_semantics=("arbitrary",), vmem_limit_bytes=VMEM_LIMIT),
        name="out_ffn_final" if final else "out_ffn",
    )(*args)


def _rope_tables(seq):
    inv_freq = 1.0 / (ROPE_THETA ** (jnp.arange(0, HEAD_DIM, 2, dtype=F32) / HEAD_DIM))
    ang = jnp.arange(seq, dtype=F32)[:, None] * inv_freq[None, :]
    cos = jnp.cos(ang)
    sin = jnp.sin(ang)
    cos_t = jnp.tile(cos, (1, LANES // (HEAD_DIM // 2)))
    sin_t = jnp.tile(jnp.concatenate([-sin, sin], axis=1), (1, LANES // HEAD_DIM))
    return cos_t, sin_t


def kernel(x, a_w_in, a_sink, a_w_out, b_w_in, b_w_out, norm_mix, norm_ffn,
           w_gate, w_up, w_down, final_norm):
    b, s, dm = x.shape
    t = b * s
    cos_t, sin_t = _rope_tables(s)
    row = lambda v: v.reshape(1, -1).astype(F32)
    bf = lambda a: a.astype(BF16)

    x2d = x.reshape(t, dm)
    qkv = _proj(x2d, row(norm_mix[0]), bf(a_w_in[0]), cos_t, sin_t, b, s, 1)
    sink_rows = jnp.repeat(a_sink[0].astype(F32), Q_BLOCK).reshape(1, N_HEADS * Q_BLOCK)
    o = _attn(qkv, 1, ATTN_HALF_WINDOW, sink_rows=sink_rows)
    x2d = _post(x2d, o.reshape(t, Q_WIDTH), bf(a_w_out[0]), row(norm_ffn[0]),
                bf(w_gate[0]), bf(w_up[0]), bf(w_down[0]))

    os_, lses, dils = [], [], []
    for gi, (window, dil) in enumerate(DILATED_GROUPS):
        w_g = bf(b_w_in[0][:, gi * QKV_WIDTH:(gi + 1) * QKV_WIDTH])
        qkv = _proj(x2d, row(norm_mix[1]), w_g, cos_t, sin_t, b, s, dil)
        o, lse = _attn(qkv, dil, window // 2 // dil)
        os_.append(o)
        lses.append(lse)
        dils.append(dil)
    o = _mix(os_, lses, dils, b, s)
    out = _post(x2d, o, bf(b_w_out[0]), row(norm_ffn[1]),
                bf(w_gate[1]), bf(w_up[1]), bf(w_down[1]), gf=row(final_norm))
    return out.reshape(b, s, dm)
```

```python
import functools
import math

import jax
import jax.numpy as jnp
from jax import lax
from jax.experimental import pallas as pl
from jax.experimental.pallas import tpu as pltpu

D_MODEL = 1024
HEAD_DIM = 64
N_HEADS = 16
N_KV_HEADS = 4
GQA = N_HEADS // N_KV_HEADS
Q_WIDTH = N_HEADS * HEAD_DIM
KV_WIDTH = N_KV_HEADS * HEAD_DIM
QKV_WIDTH = Q_WIDTH + 2 * KV_WIDTH
ATTN_HALF_WINDOW = 128
DILATED_GROUPS = ((128, 1), (512, 4), (2048, 16))
ROPE_THETA = 10000.0
RMS_EPS = 1e-6
NEG_INF = -1e30
LANES = 128
VMEM_LIMIT = 56 * 1024 * 1024

LOG2E = math.log2(math.e)
Q_SCALE = LOG2E / math.sqrt(HEAD_DIM)

BF16 = jnp.bfloat16
F32 = jnp.float32


def _rmsnorm(x, g):
    ms = jnp.mean(x * x, axis=-1, keepdims=True)
    return x * lax.rsqrt(ms + RMS_EPS) * g


def _proj_kernel(*refs, dils, ts):
    n = len(dils)
    x_ref, g_ref, w_ref, cos_ref, sin_ref = refs[:5]
    o_refs = refs[5:5 + n]
    tm = x_ref.shape[0]
    lane = lax.broadcasted_iota(jnp.int32, (ts, LANES), 1)
    first_half = (lane % HEAD_DIM) < (HEAD_DIM // 2)
    n_rope = (Q_WIDTH + KV_WIDTH) // LANES
    n_q = Q_WIDTH // LANES
    for sub in range(tm // ts):
        rows = slice(sub * ts, (sub + 1) * ts)
        h = _rmsnorm(x_ref[rows, :], g_ref[...])
        for gi, (d, o_ref) in enumerate(zip(dils, o_refs)):
            hg = h
            if d > 1:
                hg = jnp.swapaxes(h.reshape(ts // d, d, D_MODEL), 0, 1).reshape(ts, D_MODEL)
            y = jnp.dot(hg.astype(BF16), w_ref[gi], preferred_element_type=F32)
            cos = cos_ref[gi, rows, :]
            sin = sin_ref[gi, rows, :]
            for c in range(QKV_WIDTH // LANES):
                t = y[:, c * LANES:(c + 1) * LANES]
                if c < n_rope:
                    swapped = jnp.where(first_half,
                                        pltpu.roll(t, LANES - HEAD_DIM // 2, 1),
                                        pltpu.roll(t, HEAD_DIM // 2, 1))
                    t = t * cos + swapped * sin
                    if c < n_q:
                        t = t * Q_SCALE
                o_ref[0, :, sub * (ts // d):(sub + 1) * (ts // d), c * LANES:(c + 1) * LANES] = (
                    t.astype(BF16).reshape(d, ts // d, LANES))


def _tile_residue_order(table, d, tm):
    s, n = table.shape
    return table.reshape(s // tm, tm // d, d, n).swapaxes(1, 2).reshape(s, n)


def _proj(x2d, g, w, cos_t, sin_t, batch, seq, dils, tm=1024, ts=512):
    n = len(dils)
    n_seq_blocks = seq // tm
    cos_g = jnp.stack([_tile_residue_order(cos_t, d, ts) for d in dils])
    sin_g = jnp.stack([_tile_residue_order(sin_t, d, ts) for d in dils])
    table_spec = pl.BlockSpec((n, tm, LANES), lambda i: (0, i % n_seq_blocks, 0))
    outs = pl.pallas_call(
        functools.partial(_proj_kernel, dils=tuple(dils), ts=ts),
        grid=(batch * n_seq_blocks,),
        in_specs=[
            pl.BlockSpec((tm, D_MODEL), lambda i: (i, 0)),
            pl.BlockSpec((1, D_MODEL), lambda i: (0, 0)),
            pl.BlockSpec((n, D_MODEL, QKV_WIDTH), lambda i: (0, 0, 0), pipeline_mode=pl.Buffered(1)),
            table_spec, table_spec,
        ],
        out_specs=[pl.BlockSpec((1, d, tm // d, QKV_WIDTH),
                                lambda i: (i // n_seq_blocks, 0, i % n_seq_blocks, 0)) for d in dils],
        out_shape=[jax.ShapeDtypeStruct((batch, d, seq // d, QKV_WIDTH), BF16) for d in dils],
        compiler_params=pltpu.CompilerParams(
            dimension_semantics=("arbitrary",), vmem_limit_bytes=VMEM_LIMIT),
        name=f"qkv_proj_x{n}",
    )(x2d, g, w, cos_g, sin_g)
    return [o.reshape(batch, seq, QKV_WIDTH) for o in outs]


Q_BLOCK = 128


def _unit_key_start(i, qb, w, length, nk):
    return min(max(i * qb - w, 0), length - nk)


def _mask_tables(length, qb, w, nk):
    nqb = length // qb
    offs = sorted({_unit_key_start(i, qb, w, length, nk) - i * qb for i in range(nqb)})
    kpos = jnp.arange(nk)[:, None]
    qpos = jnp.arange(qb)[None, :]
    tabs = [(jnp.abs(kpos + off - qpos) <= w).astype(F32) for off in offs]
    return jnp.stack(tabs), offs


def _attn_kernel(*refs, seq, d, w, offs, has_sink):
    if has_sink:
        sink_ref, mask_ref, qkv_ref, o_ref, s_scr, m_scr = refs
        lse_ref = None
    else:
        mask_ref, qkv_ref, o_ref, lse_ref, s_scr, m_scr = refs
    qb = Q_BLOCK
    length = seq // d
    nqb = length // qb
    nk = min(qb + 2 * w, length)
    n_units = d * nqb
    nl = GQA * qb
    ones_rows = jnp.ones((16, nk), BF16)

    def unit_rows(t):
        r = t // nqb
        q0 = (t % nqb) * qb
        ks = jnp.clip(q0 - w, 0, length - nk)
        row0 = pl.multiple_of(r * length + q0, qb)
        krow0 = pl.multiple_of(r * length + ks, 16)
        return row0, krow0, ks - q0

    def scores(row0, krow0, valid4, g):
        k = qkv_ref[0, pl.ds(krow0, nk), Q_WIDTH + g * HEAD_DIM:Q_WIDTH + (g + 1) * HEAD_DIM]
        q4 = jnp.concatenate(
            [qkv_ref[0, pl.ds(row0, qb), (GQA * g + hh) * HEAD_DIM:(GQA * g + hh + 1) * HEAD_DIM]
             for hh in range(GQA)], axis=0)
        s = lax.dot_general(k, q4, (((1,), (1,)), ((), ())), preferred_element_type=F32)
        s = jnp.where(valid4, s, NEG_INF)
        m = jnp.max(s, axis=0, keepdims=True)
        if has_sink:
            m = jnp.maximum(m, sink_ref[:, g * nl:(g + 1) * nl])
        return s, m

    def finish(row0, krow0, g, s, m):
        p = jnp.exp2(s - m).astype(BF16)
        v = qkv_ref[0, pl.ds(krow0, nk),
                    Q_WIDTH + KV_WIDTH + g * HEAD_DIM:Q_WIDTH + KV_WIDTH + (g + 1) * HEAD_DIM]
        lhs = jnp.concatenate([v.T, ones_rows], axis=0)
        res = jnp.dot(lhs, p, preferred_element_type=F32)
        l = res[HEAD_DIM:HEAD_DIM + 1]
        if has_sink:
            l = l + jnp.exp2(sink_ref[:, g * nl:(g + 1) * nl] - m)
        o_t = res[:HEAD_DIM] * (1.0 / l)
        for hp in range(GQA // 2):
            pair = jnp.concatenate([o_t[:, (2 * hp) * qb:(2 * hp + 1) * qb],
                                    o_t[:, (2 * hp + 1) * qb:(2 * hp + 2) * qb]], axis=0)
            c0 = (GQA * g + 2 * hp) * HEAD_DIM
            o_ref[0, pl.ds(row0, qb), c0:c0 + 2 * HEAD_DIM] = pair.T.astype(BF16)
        if lse_ref is None:
            return []
        lse = (m + jnp.log2(l)) * (1.0 / LOG2E)
        return [lse[:, hh * qb:(hh + 1) * qb] for hh in range(GQA)]

    def unit_mask(off):
        table = sum((off > o).astype(jnp.int32) for o in offs[:-1]) if len(offs) > 1 else 0
        valid = mask_ref[table] > 0.5
        return jnp.concatenate([valid] * GQA, axis=1)

    def step(t_cur, t_next, slot_cur, slot_next):
        nrow0, nkrow0, noff = unit_rows(t_next)
        nvalid4 = unit_mask(noff)
        if t_cur is None:
            for g in range(N_KV_HEADS):
                s, m = scores(nrow0, nkrow0, nvalid4, g)
                s_scr[slot_next, g] = s
                m_scr[slot_next, g] = m
            return
        row0, krow0, _ = unit_rows(t_cur)
        lse_rows = []
        for g in range(N_KV_HEADS):
            s, m = scores(nrow0, nkrow0, nvalid4, g)
            s_scr[slot_next, g] = s
            m_scr[slot_next, g] = m
            lse_rows += finish(row0, krow0, g, s_scr[slot_cur, g], m_scr[slot_cur, g])
        if lse_ref is not None:
            pad = jnp.zeros((LANES - N_HEADS, qb), F32)
            lse_ref[0, pl.ds(row0, qb), :] = jnp.concatenate(lse_rows + [pad], axis=0).T

    step(None, 0, None, 0)

    def body(tt, carry):
        t = 2 * tt
        step(t, t + 1, 0, 1)
        step(t + 1, jnp.minimum(t + 2, n_units - 1), 1, 0)
        return carry

    lax.fori_loop(0, n_units // 2, body, 0)


def _attn(qkv, d, w, sink_rows=None):
    b, seq, _ = qkv.shape
    has_sink = sink_rows is not None
    length = seq // d
    nk = min(Q_BLOCK + 2 * w, length)
    masks, offs = _mask_tables(length, Q_BLOCK, w, nk)
    kern = functools.partial(_attn_kernel, seq=seq, d=d, w=w, offs=tuple(offs), has_sink=has_sink)
    in_specs = [pl.BlockSpec(masks.shape, lambda i: (0, 0, 0)),
                pl.BlockSpec((1, seq, QKV_WIDTH), lambda i: (i, 0, 0))]
    args = [masks, qkv]
    o_spec = pl.BlockSpec((1, seq, Q_WIDTH), lambda i: (i, 0, 0))
    o_shape = jax.ShapeDtypeStruct((b, seq, Q_WIDTH), BF16)
    if has_sink:
        in_specs = [pl.BlockSpec((1, N_HEADS * Q_BLOCK), lambda i: (0, 0))] + in_specs
        args = [sink_rows] + args
        out_specs, out_shape = o_spec, o_shape
    else:
        out_specs = (o_spec, pl.BlockSpec((1, seq, LANES), lambda i: (i, 0, 0)))
        out_shape = (o_shape, jax.ShapeDtypeStruct((b, seq, LANES), F32))
    return pl.pallas_call(
        kern,
        grid=(b,),
        in_specs=in_specs,
        out_specs=out_specs,
        out_shape=out_shape,
        scratch_shapes=[pltpu.VMEM((2, N_KV_HEADS, nk, GQA * Q_BLOCK), F32),
                        pltpu.VMEM((2, N_KV_HEADS, 1, GQA * Q_BLOCK), F32)],
        compiler_params=pltpu.CompilerParams(
            dimension_semantics=("arbitrary",), vmem_limit_bytes=VMEM_LIMIT),
        name=f"band_attn_d{d}",
    )(*args)


def _mix_kernel(*refs, dils):
    n = len(dils)
    o_refs, lse_refs = refs[:n], refs[n:2 * n]
    e_ref, out_ref = refs[2 * n:]
    tm = out_ref.shape[0]

    def token_order(ref, d):
        a = ref[0].astype(F32)
        if d > 1:
            a = jnp.swapaxes(a, 0, 1)
        return a.reshape(tm, a.shape[-1])

    lses = [token_order(r, d) for r, d in zip(lse_refs, dils)]
    mx = functools.reduce(jnp.maximum, lses)
    es = [jnp.exp(l - mx) for l in lses]
    inv = 1.0 / functools.reduce(jnp.add, es)
    acc = jnp.zeros((tm, Q_WIDTH), F32)
    for e, o_ref, d in zip(es, o_refs, dils):
        wt = e * inv
        hi = wt.astype(BF16)
        lo = (wt - hi.astype(F32)).astype(BF16)
        wfull = jnp.dot(jnp.concatenate([hi, lo], axis=1), e_ref[...], preferred_element_type=F32)
        acc = acc + wfull * token_order(o_ref, d)
    out_ref[...] = acc.astype(BF16)


def _mix(os_, lses, dils, batch, seq, tm=512):
    n_seq_blocks = seq // tm
    head_of_lane = jnp.arange(Q_WIDTH) // HEAD_DIM
    spread = (jnp.arange(LANES)[:, None] == head_of_lane[None, :]).astype(BF16)
    spread = jnp.concatenate([spread, spread], axis=0)
    idx = lambda i: (i // n_seq_blocks, 0, i % n_seq_blocks, 0)
    in_specs, args = [], []
    for a, width in ((os_, Q_WIDTH), (lses, LANES)):
        for arr, d in zip(a, dils):
            in_specs.append(pl.BlockSpec((1, d, tm // d, width), idx))
            args.append(arr.reshape(batch, d, seq // d, width))
    in_specs.append(pl.BlockSpec((2 * LANES, Q_WIDTH), lambda i: (0, 0)))
    return pl.pallas_call(
        functools.partial(_mix_kernel, dils=tuple(dils)),
        grid=(batch * n_seq_blocks,),
        in_specs=in_specs,
        out_specs=pl.BlockSpec((tm, Q_WIDTH), lambda i: (i, 0)),
        out_shape=jax.ShapeDtypeStruct((batch * seq, Q_WIDTH), BF16),
        compiler_params=pltpu.CompilerParams(
            dimension_semantics=("arbitrary",), vmem_limit_bytes=VMEM_LIMIT),
        name="group_mix",
    )(*args, spread)


def _post_kernel(*refs, final):
    x_ref, o_ref, wo_ref, g_ref, wg_ref, wu_ref, wd_ref = refs[:7]
    gf_ref = refs[7] if final else None
    out_ref = refs[-1]

    x1 = x_ref[...] + jnp.dot(o_ref[...], wo_ref[...], preferred_element_type=F32)
    h = _rmsnorm(x1, g_ref[...]).astype(BF16)
    gate = jnp.dot(h, wg_ref[...], preferred_element_type=F32)
    up = jnp.dot(h, wu_ref[...], preferred_element_type=F32)
    act = (gate * jax.nn.sigmoid(gate) * up).astype(BF16)
    x2 = x1 + jnp.dot(act, wd_ref[...], preferred_element_type=F32)
    if final:
        x2 = _rmsnorm(x2, gf_ref[...])
    out_ref[...] = x2


def _post(x2d, o2d, wo, g, wg, wu, wd, gf=None, tm=512):
    t = x2d.shape[0]
    final = gf is not None
    d_ff = wg.shape[1]
    tile = lambda i: (i, 0)
    const = lambda i: (0, 0)
    resident = functools.partial(pl.BlockSpec, index_map=const, pipeline_mode=pl.Buffered(1))
    in_specs = [pl.BlockSpec((tm, D_MODEL), tile), pl.BlockSpec((tm, Q_WIDTH), tile),
                resident((Q_WIDTH, D_MODEL)), resident((1, D_MODEL)),
                resident((D_MODEL, d_ff)), resident((D_MODEL, d_ff)), resident((d_ff, D_MODEL))]
    args = [x2d, o2d, wo, g, wg, wu, wd]
    if final:
        in_specs.append(resident((1, D_MODEL)))
        args.append(gf)
    return pl.pallas_call(
        functools.partial(_post_kernel, final=final),
        grid=(t // tm,),
        in_specs=in_specs,
        out_specs=pl.BlockSpec((tm, D_MODEL), tile),
        out_shape=jax.ShapeDtypeStruct((t, D_MODEL), F32),
        compiler_params=pltpu.CompilerParams(
            dimension_semantics=("arbitrary",), vmem_limit_bytes=VMEM_LIMIT),
        name="out_ffn_final" if final else "out_ffn",
    )(*args)


def _rope_tables(seq):
    inv_freq = 1.0 / (ROPE_THETA ** (jnp.arange(0, HEAD_DIM, 2, dtype=F32) / HEAD_DIM))
    ang = jnp.arange(seq, dtype=F32)[:, None] * inv_freq[None, :]
    cos = jnp.cos(ang)
    sin = jnp.sin(ang)
    cos_t = jnp.tile(cos, (1, LANES // (HEAD_DIM // 2)))
    sin_t = jnp.tile(jnp.concatenate([-sin, sin], axis=1), (1, LANES // HEAD_DIM))
    return cos_t, sin_t


def kernel(x, a_w_in, a_sink, a_w_out, b_w_in, b_w_out, norm_mix, norm_ffn,
           w_gate, w_up, w_down, final_norm):
    b, s, dm = x.shape
    t = b * s
    cos_t, sin_t = _rope_tables(s)
    row = lambda v: v.reshape(1, -1).astype(F32)
    bf = lambda a: a.astype(BF16)

    x2d = x.reshape(t, dm)
    (qkv,) = _proj(x2d, row(norm_mix[0]), bf(a_w_in[0])[None], cos_t, sin_t, b, s, (1,))
    sink_rows = jnp.repeat(a_sink[0].astype(F32) * LOG2E, Q_BLOCK).reshape(1, N_HEADS * Q_BLOCK)
    o = _attn(qkv, 1, ATTN_HALF_WINDOW, sink_rows=sink_rows)
    x2d = _post(x2d, o.reshape(t, Q_WIDTH), bf(a_w_out[0]), row(norm_ffn[0]),
                bf(w_gate[0]), bf(w_up[0]), bf(w_down[0]))

    dils = tuple(dil for _, dil in DILATED_GROUPS)
    w_groups = bf(b_w_in[0]).reshape(dm, len(dils), QKV_WIDTH).swapaxes(0, 1)
    qkvs = _proj(x2d, row(norm_mix[1]), w_groups, cos_t, sin_t, b, s, dils)
    os_, lses = [], []
    for qkv, (window, dil) in zip(qkvs, DILATED_GROUPS):
        o, lse = _attn(qkv, dil, window // 2 // dil)
        os_.append(o)
        lses.append(lse)
    o = _mix(os_, lses, dils, b, s)
    out = _post(x2d, o, bf(b_w_out[0]), row(norm_ffn[1]),
                bf(w_gate[1]), bf(w_up[1]), bf(w_down[1]), gf=row(final_norm))
    return out.reshape(b, s, dm)
```

```python
import functools
import math

import jax
import jax.numpy as jnp
import numpy as np
from jax import lax
from jax.experimental import pallas as pl
from jax.experimental.pallas import tpu as pltpu

D_MODEL = 1024
HEAD_DIM = 64
N_HEADS = 16
N_KV_HEADS = 4
GQA = N_HEADS // N_KV_HEADS
Q_WIDTH = N_HEADS * HEAD_DIM
KV_WIDTH = N_KV_HEADS * HEAD_DIM
QKV_WIDTH = Q_WIDTH + 2 * KV_WIDTH
ATTN_HALF_WINDOW = 128
DILATED_GROUPS = ((128, 1), (512, 4), (2048, 16))
ROPE_THETA = 10000.0
RMS_EPS = 1e-6
NEG_INF = -1e30
LANES = 128
VMEM_LIMIT = 56 * 1024 * 1024

LOG2E = math.log2(math.e)
Q_SCALE = LOG2E / math.sqrt(HEAD_DIM)

BF16 = jnp.bfloat16
F32 = jnp.float32


def _rmsnorm(x, g):
    ms = jnp.mean(x * x, axis=-1, keepdims=True)
    return x * lax.rsqrt(ms + RMS_EPS) * g


def _proj_kernel(*refs, dils, ts):
    n = len(dils)
    x_ref, g_ref, w_ref, cos_ref, sin_ref = refs[:5]
    o_refs = refs[5:5 + n]
    tm = x_ref.shape[0]
    lane = lax.broadcasted_iota(jnp.int32, (ts, LANES), 1)
    first_half = (lane % HEAD_DIM) < (HEAD_DIM // 2)
    n_rope = (Q_WIDTH + KV_WIDTH) // LANES
    n_q = Q_WIDTH // LANES
    for sub in range(tm // ts):
        rows = slice(sub * ts, (sub + 1) * ts)
        h = _rmsnorm(x_ref[rows, :], g_ref[...])
        for gi, (d, o_ref) in enumerate(zip(dils, o_refs)):
            hg = h
            if d > 1:
                hg = jnp.swapaxes(h.reshape(ts // d, d, D_MODEL), 0, 1).reshape(ts, D_MODEL)
            y = jnp.dot(hg.astype(BF16), w_ref[:, gi * QKV_WIDTH:(gi + 1) * QKV_WIDTH],
                        preferred_element_type=F32)
            cos = cos_ref[gi, rows, :]
            sin = sin_ref[gi, rows, :]
            for c in range(QKV_WIDTH // LANES):
                t = y[:, c * LANES:(c + 1) * LANES]
                if c < n_rope:
                    swapped = jnp.where(first_half,
                                        pltpu.roll(t, LANES - HEAD_DIM // 2, 1),
                                        pltpu.roll(t, HEAD_DIM // 2, 1))
                    t = t * cos + swapped * sin
                    if c < n_q:
                        t = t * Q_SCALE
                o_ref[0, :, sub * (ts // d):(sub + 1) * (ts // d), c * LANES:(c + 1) * LANES] = (
                    t.astype(BF16).reshape(d, ts // d, LANES))


def _tile_residue_order(table, d, tm):
    s, n = table.shape
    return table.reshape(s // tm, tm // d, d, n).swapaxes(1, 2).reshape(s, n)


def _proj(x2d, g, w, cos_t, sin_t, batch, seq, dils, tm=1024, ts=512):
    n = len(dils)
    n_seq_blocks = seq // tm
    cos_g = np.stack([_tile_residue_order(cos_t, d, ts) for d in dils])
    sin_g = np.stack([_tile_residue_order(sin_t, d, ts) for d in dils])
    table_spec = pl.BlockSpec((n, tm, LANES), lambda i: (0, i % n_seq_blocks, 0))
    outs = pl.pallas_call(
        functools.partial(_proj_kernel, dils=tuple(dils), ts=ts),
        grid=(batch * n_seq_blocks,),
        in_specs=[
            pl.BlockSpec((tm, D_MODEL), lambda i: (i, 0)),
            pl.BlockSpec((1, D_MODEL), lambda i: (0, 0)),
            pl.BlockSpec((D_MODEL, n * QKV_WIDTH), lambda i: (0, 0), pipeline_mode=pl.Buffered(1)),
            table_spec, table_spec,
        ],
        out_specs=[pl.BlockSpec((1, d, tm // d, QKV_WIDTH),
                                lambda i: (i // n_seq_blocks, 0, i % n_seq_blocks, 0)) for d in dils],
        out_shape=[jax.ShapeDtypeStruct((batch, d, seq // d, QKV_WIDTH), BF16) for d in dils],
        compiler_params=pltpu.CompilerParams(
            dimension_semantics=("arbitrary",), vmem_limit_bytes=VMEM_LIMIT),
        name=f"qkv_proj_x{n}",
    )(x2d, g, w, cos_g, sin_g)
    return [o.reshape(batch, seq, QKV_WIDTH) for o in outs]


Q_BLOCK = 128
UNITS_PER_TRIP = 4


def _unit_key_start(i, qb, w, length, nk):
    return min(max(i * qb - w, 0), length - nk)


def _mask_tables(length, qb, w, nk):
    nqb = length // qb
    offs = sorted({_unit_key_start(i, qb, w, length, nk) - i * qb for i in range(nqb)})
    kpos = np.arange(nk)[:, None]
    qpos = np.arange(qb)[None, :]
    tabs = [(np.abs(kpos + off - qpos) <= w).astype(np.float32) for off in offs]
    return np.stack(tabs), offs


def _attn_kernel(*refs, seq, d, w, offs, has_sink):
    if has_sink:
        sink_ref, mask_ref, qkv_ref, o_ref, s_scr, m_scr = refs
        lse_ref = None
    else:
        mask_ref, qkv_ref, o_ref, lse_ref, s_scr, m_scr = refs
    qb = Q_BLOCK
    length = seq // d
    nqb = length // qb
    nk = min(qb + 2 * w, length)
    n_units = d * nqb
    nl = GQA * qb
    ones_rows = jnp.ones((16, nk), BF16)

    def unit_rows(t):
        r = t // nqb
        q0 = (t % nqb) * qb
        ks = jnp.clip(q0 - w, 0, length - nk)
        row0 = pl.multiple_of(r * length + q0, qb)
        krow0 = pl.multiple_of(r * length + ks, 16)
        return row0, krow0, ks - q0

    def scores(row0, krow0, valid4, g):
        k = qkv_ref[0, pl.ds(krow0, nk), Q_WIDTH + g * HEAD_DIM:Q_WIDTH + (g + 1) * HEAD_DIM]
        q4 = jnp.concatenate(
            [qkv_ref[0, pl.ds(row0, qb), (GQA * g + hh) * HEAD_DIM:(GQA * g + hh + 1) * HEAD_DIM]
             for hh in range(GQA)], axis=0)
        s = lax.dot_general(k, q4, (((1,), (1,)), ((), ())), preferred_element_type=F32)
        s = jnp.where(valid4, s, NEG_INF)
        m = jnp.max(s, axis=0, keepdims=True)
        if has_sink:
            m = jnp.maximum(m, sink_ref[:, g * nl:(g + 1) * nl])
        return s, m

    def finish(row0, krow0, g, s, m):
        p = jnp.exp2(s - m).astype(BF16)
        v = qkv_ref[0, pl.ds(krow0, nk),
                    Q_WIDTH + KV_WIDTH + g * HEAD_DIM:Q_WIDTH + KV_WIDTH + (g + 1) * HEAD_DIM]
        lhs = jnp.concatenate([v.T, ones_rows], axis=0)
        res = jnp.dot(lhs, p, preferred_element_type=F32)
        l = res[HEAD_DIM:HEAD_DIM + 1]
        if has_sink:
            l = l + jnp.exp2(sink_ref[:, g * nl:(g + 1) * nl] - m)
        o_t = res[:HEAD_DIM] * (1.0 / l)
        for hp in range(GQA // 2):
            pair = jnp.concatenate([o_t[:, (2 * hp) * qb:(2 * hp + 1) * qb],
                                    o_t[:, (2 * hp + 1) * qb:(2 * hp + 2) * qb]], axis=0)
            c0 = (GQA * g + 2 * hp) * HEAD_DIM
            o_ref[0, pl.ds(row0, qb), c0:c0 + 2 * HEAD_DIM] = pair.T.astype(BF16)
        if lse_ref is None:
            return []
        lse = (m + jnp.log2(l)) * (1.0 / LOG2E)
        return [lse[:, hh * qb:(hh + 1) * qb] for hh in range(GQA)]

    def unit_mask(off):
        table = sum((off > o).astype(jnp.int32) for o in offs[:-1]) if len(offs) > 1 else 0
        valid = mask_ref[table] > 0.5
        return jnp.concatenate([valid] * GQA, axis=1)

    def step(t_cur, t_next, slot_cur, slot_next):
        nrow0, nkrow0, noff = unit_rows(t_next)
        nvalid4 = unit_mask(noff)
        if t_cur is None:
            for g in range(N_KV_HEADS):
                s, m = scores(nrow0, nkrow0, nvalid4, g)
                s_scr[slot_next, g] = s
                m_scr[slot_next, g] = m
            return
        row0, krow0, _ = unit_rows(t_cur)
        lse_rows = []
        for g in range(N_KV_HEADS):
            s, m = scores(nrow0, nkrow0, nvalid4, g)
            s_scr[slot_next, g] = s
            m_scr[slot_next, g] = m
            lse_rows += finish(row0, krow0, g, s_scr[slot_cur, g], m_scr[slot_cur, g])
        if lse_ref is not None:
            pad = jnp.zeros((LANES - N_HEADS, qb), F32)
            lse_ref[0, pl.ds(row0, qb), :] = jnp.concatenate(lse_rows + [pad], axis=0).T

    step(None, 0, None, 0)

    def body(tt, carry):
        t = UNITS_PER_TRIP * tt
        for u in range(UNITS_PER_TRIP):
            step(t + u, jnp.minimum(t + u + 1, n_units - 1), u % 2, (u + 1) % 2)
        return carry

    lax.fori_loop(0, n_units // UNITS_PER_TRIP, body, 0)


def _attn(qkv, d, w, sink_rows=None):
    b, seq, _ = qkv.shape
    has_sink = sink_rows is not None
    length = seq // d
    nk = min(Q_BLOCK + 2 * w, length)
    masks, offs = _mask_tables(length, Q_BLOCK, w, nk)
    kern = functools.partial(_attn_kernel, seq=seq, d=d, w=w, offs=tuple(offs), has_sink=has_sink)
    in_specs = [pl.BlockSpec(masks.shape, lambda i: (0, 0, 0)),
                pl.BlockSpec((1, seq, QKV_WIDTH), lambda i: (i, 0, 0))]
    args = [masks, qkv]
    o_spec = pl.BlockSpec((1, seq, Q_WIDTH), lambda i: (i, 0, 0))
    o_shape = jax.ShapeDtypeStruct((b, seq, Q_WIDTH), BF16)
    if has_sink:
        in_specs = [pl.BlockSpec((1, N_HEADS * Q_BLOCK), lambda i: (0, 0))] + in_specs
        args = [sink_rows] + args
        out_specs, out_shape = o_spec, o_shape
    else:
        out_specs = (o_spec, pl.BlockSpec((1, seq, LANES), lambda i: (i, 0, 0)))
        out_shape = (o_shape, jax.ShapeDtypeStruct((b, seq, LANES), F32))
    return pl.pallas_call(
        kern,
        grid=(b,),
        in_specs=in_specs,
        out_specs=out_specs,
        out_shape=out_shape,
        scratch_shapes=[pltpu.VMEM((2, N_KV_HEADS, nk, GQA * Q_BLOCK), F32),
                        pltpu.VMEM((2, N_KV_HEADS, 1, GQA * Q_BLOCK), F32)],
        compiler_params=pltpu.CompilerParams(
            dimension_semantics=("arbitrary",), vmem_limit_bytes=VMEM_LIMIT),
        name=f"band_attn_d{d}",
    )(*args)


def _mix_kernel(*refs, dils):
    n = len(dils)
    o_refs, lse_refs = refs[:n], refs[n:2 * n]
    e_ref, out_ref = refs[2 * n:]
    tm = out_ref.shape[0]

    def token_order(ref, d):
        a = ref[0].astype(F32)
        if d > 1:
            a = jnp.swapaxes(a, 0, 1)
        return a.reshape(tm, a.shape[-1])

    lses = [token_order(r, d) for r, d in zip(lse_refs, dils)]
    mx = functools.reduce(jnp.maximum, lses)
    es = [jnp.exp(l - mx) for l in lses]
    inv = 1.0 / functools.reduce(jnp.add, es)
    acc = jnp.zeros((tm, Q_WIDTH), F32)
    for e, o_ref, d in zip(es, o_refs, dils):
        wt = e * inv
        hi = wt.astype(BF16)
        lo = (wt - hi.astype(F32)).astype(BF16)
        wfull = jnp.dot(jnp.concatenate([hi, lo], axis=1), e_ref[...], preferred_element_type=F32)
        acc = acc + wfull * token_order(o_ref, d)
    out_ref[...] = acc.astype(BF16)


def _mix(os_, lses, dils, batch, seq, tm=1024):
    n_seq_blocks = seq // tm
    head_of_lane = np.arange(Q_WIDTH) // HEAD_DIM
    spread = (np.arange(LANES)[:, None] == head_of_lane[None, :]).astype(np.float32)
    spread = jnp.asarray(np.concatenate([spread, spread], axis=0), BF16)
    idx = lambda i: (i // n_seq_blocks, 0, i % n_seq_blocks, 0)
    in_specs, args = [], []
    for a, width in ((os_, Q_WIDTH), (lses, LANES)):
        for arr, d in zip(a, dils):
            in_specs.append(pl.BlockSpec((1, d, tm // d, width), idx))
            args.append(arr.reshape(batch, d, seq // d, width))
    in_specs.append(pl.BlockSpec((2 * LANES, Q_WIDTH), lambda i: (0, 0)))
    return pl.pallas_call(
        functools.partial(_mix_kernel, dils=tuple(dils)),
        grid=(batch * n_seq_blocks,),
        in_specs=in_specs,
        out_specs=pl.BlockSpec((tm, Q_WIDTH), lambda i: (i, 0)),
        out_shape=jax.ShapeDtypeStruct((batch * seq, Q_WIDTH), BF16),
        compiler_params=pltpu.CompilerParams(
            dimension_semantics=("arbitrary",), vmem_limit_bytes=VMEM_LIMIT),
        name="group_mix",
    )(*args, spread)


def _post_kernel(*refs, final, ts):
    x_ref, o_ref, wo_ref, g_ref, wg_ref, wu_ref, wd_ref = refs[:7]
    gf_ref = refs[7] if final else None
    out_ref = refs[-1]
    tm = x_ref.shape[0]
    for sub in range(tm // ts):
        rows = slice(sub * ts, (sub + 1) * ts)
        x1 = x_ref[rows, :] + jnp.dot(o_ref[rows, :], wo_ref[...], preferred_element_type=F32)
        h = _rmsnorm(x1, g_ref[...]).astype(BF16)
        gate = jnp.dot(h, wg_ref[...], preferred_element_type=F32)
        up = jnp.dot(h, wu_ref[...], preferred_element_type=F32)
        act = (gate * jax.nn.sigmoid(gate) * up).astype(BF16)
        x2 = x1 + jnp.dot(act, wd_ref[...], preferred_element_type=F32)
        if final:
            x2 = _rmsnorm(x2, gf_ref[...])
        out_ref[rows, :] = x2


def _post(x2d, o2d, wo, g, wg, wu, wd, gf=None, tm=1024, ts=512):
    t = x2d.shape[0]
    final = gf is not None
    d_ff = wg.shape[1]
    tile = lambda i: (i, 0)
    const = lambda i: (0, 0)
    resident = functools.partial(pl.BlockSpec, index_map=const, pipeline_mode=pl.Buffered(1))
    in_specs = [pl.BlockSpec((tm, D_MODEL), tile), pl.BlockSpec((tm, Q_WIDTH), tile),
                resident((Q_WIDTH, D_MODEL)), resident((1, D_MODEL)),
                resident((D_MODEL, d_ff)), resident((D_MODEL, d_ff)), resident((d_ff, D_MODEL))]
    args = [x2d, o2d, wo, g, wg, wu, wd]
    if final:
        in_specs.append(resident((1, D_MODEL)))
        args.append(gf)
    return pl.pallas_call(
        functools.partial(_post_kernel, final=final, ts=ts),
        grid=(t // tm,),
        in_specs=in_specs,
        out_specs=pl.BlockSpec((tm, D_MODEL), tile),
        out_shape=jax.ShapeDtypeStruct((t, D_MODEL), F32),
        compiler_params=pltpu.CompilerParams(
            dimension_semantics=("arbitrary",), vmem_limit_bytes=VMEM_LIMIT),
        name="out_ffn_final" if final else "out_ffn",
    )(*args)


def _rope_tables(seq):
    f32 = np.float32
    inv_freq = (f32(1.0) / (f32(ROPE_THETA) ** (np.arange(0, HEAD_DIM, 2, dtype=f32) / f32(HEAD_DIM))))
    ang = np.arange(seq, dtype=f32)[:, None] * inv_freq[None, :].astype(f32)
    cos = np.cos(ang).astype(f32)
    sin = np.sin(ang).astype(f32)
    cos_t = np.tile(cos, (1, LANES // (HEAD_DIM // 2)))
    sin_t = np.tile(np.concatenate([-sin, sin], axis=1), (1, LANES // HEAD_DIM))
    return cos_t, sin_t


def kernel(x, a_w_in, a_sink, a_w_out, b_w_in, b_w_out, norm_mix, norm_ffn,
           w_gate, w_up, w_down, final_norm):
    b, s, dm = x.shape
    t = b * s
    cos_t, sin_t = _rope_tables(s)
    row = lambda v: v.reshape(1, -1).astype(F32)
    bf = lambda a: a.astype(BF16)

    x2d = x.reshape(t, dm)
    (qkv,) = _proj(x2d, row(norm_mix[0]), bf(a_w_in[0]), cos_t, sin_t, b, s, (1,))
    sink_rows = jnp.repeat(a_sink[0].astype(F32) * LOG2E, Q_BLOCK).reshape(1, N_HEADS * Q_BLOCK)
    o = _attn(qkv, 1, ATTN_HALF_WINDOW, sink_rows=sink_rows)
    x2d = _post(x2d, o.reshape(t, Q_WIDTH), bf(a_w_out[0]), row(norm_ffn[0]),
                bf(w_gate[0]), bf(w_up[0]), bf(w_down[0]))

    dils = tuple(dil for _, dil in DILATED_GROUPS)
    qkvs = _proj(x2d, row(norm_mix[1]), bf(b_w_in[0]), cos_t, sin_t, b, s, dils)
    os_, lses = [], []
    for qkv, (window, dil) in zip(qkvs, DILATED_GROUPS):
        o, lse = _attn(qkv, dil, window // 2 // dil)
        os_.append(o)
        lses.append(lse)
    o = _mix(os_, lses, dils, b, s)
    out = _post(x2d, o, bf(b_w_out[0]), row(norm_ffn[1]),
                bf(w_gate[1]), bf(w_up[1]), bf(w_down[1]), gf=row(final_norm))
    return out.reshape(b, s, dm)
```

```python
import functools
import math

import jax
import jax.numpy as jnp
import numpy as np
from jax import lax
from jax.experimental import pallas as pl
from jax.experimental.pallas import tpu as pltpu

D_MODEL = 1024
HEAD_DIM = 64
N_HEADS = 16
N_KV_HEADS = 4
GQA = N_HEADS // N_KV_HEADS
Q_WIDTH = N_HEADS * HEAD_DIM
KV_WIDTH = N_KV_HEADS * HEAD_DIM
QKV_WIDTH = Q_WIDTH + 2 * KV_WIDTH
ATTN_HALF_WINDOW = 128
DILATED_GROUPS = ((128, 1), (512, 4), (2048, 16))
ROPE_THETA = 10000.0
RMS_EPS = 1e-6
NEG_INF = -1e30
LANES = 128
VMEM_LIMIT = 56 * 1024 * 1024

LOG2E = math.log2(math.e)
Q_SCALE = LOG2E / math.sqrt(HEAD_DIM)

BF16 = jnp.bfloat16
F32 = jnp.float32


def _rmsnorm(x, g):
    ms = jnp.mean(x * x, axis=-1, keepdims=True)
    return x * lax.rsqrt(ms + RMS_EPS) * g


def _proj_kernel(*refs, dils, ts):
    n = len(dils)
    x_ref, g_ref, w_ref, cos_ref, sin_ref = refs[:5]
    o_refs = refs[5:5 + n]
    tm = x_ref.shape[0]
    lane = lax.broadcasted_iota(jnp.int32, (ts, LANES), 1)
    first_half = (lane % HEAD_DIM) < (HEAD_DIM // 2)
    n_rope = (Q_WIDTH + KV_WIDTH) // LANES
    n_q = Q_WIDTH // LANES
    for sub in range(tm // ts):
        rows = slice(sub * ts, (sub + 1) * ts)
        h = _rmsnorm(x_ref[rows, :], g_ref[...])
        for gi, (d, o_ref) in enumerate(zip(dils, o_refs)):
            hg = h
            if d > 1:
                hg = jnp.swapaxes(h.reshape(ts // d, d, D_MODEL), 0, 1).reshape(ts, D_MODEL)
            y = jnp.dot(hg.astype(BF16), w_ref[:, gi * QKV_WIDTH:(gi + 1) * QKV_WIDTH],
                        preferred_element_type=F32)
            cos = cos_ref[gi, rows, :]
            sin = sin_ref[gi, rows, :]
            for c in range(QKV_WIDTH // LANES):
                t = y[:, c * LANES:(c + 1) * LANES]
                if c < n_rope:
                    swapped = jnp.where(first_half,
                                        pltpu.roll(t, LANES - HEAD_DIM // 2, 1),
                                        pltpu.roll(t, HEAD_DIM // 2, 1))
                    t = t * cos + swapped * sin
                    if c < n_q:
                        t = t * Q_SCALE
                o_ref[0, :, sub * (ts // d):(sub + 1) * (ts // d), c * LANES:(c + 1) * LANES] = (
                    t.astype(BF16).reshape(d, ts // d, LANES))


def _tile_residue_order(table, d, tm):
    s, n = table.shape
    return table.reshape(s // tm, tm // d, d, n).swapaxes(1, 2).reshape(s, n)


def _proj(x2d, g, w, cos_t, sin_t, batch, seq, dils, tm=1024, ts=512):
    n = len(dils)
    n_seq_blocks = seq // tm
    cos_g = np.stack([_tile_residue_order(cos_t, d, ts) for d in dils])
    sin_g = np.stack([_tile_residue_order(sin_t, d, ts) for d in dils])
    table_spec = pl.BlockSpec((n, tm, LANES), lambda i: (0, i % n_seq_blocks, 0))
    outs = pl.pallas_call(
        functools.partial(_proj_kernel, dils=tuple(dils), ts=ts),
        grid=(batch * n_seq_blocks,),
        in_specs=[
            pl.BlockSpec((tm, D_MODEL), lambda i: (i, 0)),
            pl.BlockSpec((1, D_MODEL), lambda i: (0, 0)),
            pl.BlockSpec((D_MODEL, n * QKV_WIDTH), lambda i: (0, 0), pipeline_mode=pl.Buffered(1)),
            table_spec, table_spec,
        ],
        out_specs=[pl.BlockSpec((1, d, tm // d, QKV_WIDTH),
                                lambda i: (i // n_seq_blocks, 0, i % n_seq_blocks, 0)) for d in dils],
        out_shape=[jax.ShapeDtypeStruct((batch, d, seq // d, QKV_WIDTH), BF16) for d in dils],
        compiler_params=pltpu.CompilerParams(
            dimension_semantics=("arbitrary",), vmem_limit_bytes=VMEM_LIMIT),
        name=f"qkv_proj_x{n}",
    )(x2d, g, w, cos_g, sin_g)
    return [o.reshape(batch, seq, QKV_WIDTH) for o in outs]


Q_BLOCK = 128
MAX_UNITS_PER_TRIP = 8


def _unit_key_start(i, qb, w, length, nk):
    return min(max(i * qb - w, 0), length - nk)


def _mask_tables(length, qb, w, nk):
    nqb = length // qb
    offs = sorted({_unit_key_start(i, qb, w, length, nk) - i * qb for i in range(nqb)})
    kpos = np.arange(nk)[:, None]
    qpos = np.arange(qb)[None, :]
    tabs = [(np.abs(kpos + off - qpos) <= w).astype(np.float32) for off in offs]
    return np.stack(tabs), offs


def _attn_kernel(*refs, seq, d, w, offs, units_per_trip, has_sink):
    if has_sink:
        sink_ref, mask_ref, qkv_ref, o_ref, s_scr, m_scr = refs
        lse_ref = None
    else:
        mask_ref, qkv_ref, o_ref, lse_ref, s_scr, m_scr = refs
    qb = Q_BLOCK
    length = seq // d
    nqb = length // qb
    nk = min(qb + 2 * w, length)
    n_units = d * nqb
    nl = GQA * qb
    ones_rows = jnp.ones((16, nk), BF16)

    def unit_rows(t):
        r = t // nqb
        q0 = (t % nqb) * qb
        ks = jnp.clip(q0 - w, 0, length - nk)
        row0 = pl.multiple_of(r * length + q0, qb)
        krow0 = pl.multiple_of(r * length + ks, 16)
        return row0, krow0, ks - q0

    def scores(row0, krow0, valid4, g):
        k = qkv_ref[0, pl.ds(krow0, nk), Q_WIDTH + g * HEAD_DIM:Q_WIDTH + (g + 1) * HEAD_DIM]
        q4 = jnp.concatenate(
            [qkv_ref[0, pl.ds(row0, qb), (GQA * g + hh) * HEAD_DIM:(GQA * g + hh + 1) * HEAD_DIM]
             for hh in range(GQA)], axis=0)
        s = lax.dot_general(k, q4, (((1,), (1,)), ((), ())), preferred_element_type=F32)
        s = jnp.where(valid4, s, NEG_INF)
        m = jnp.max(s, axis=0, keepdims=True)
        if has_sink:
            m = jnp.maximum(m, sink_ref[:, g * nl:(g + 1) * nl])
        return s, m

    def finish(row0, krow0, g, s, m):
        p = jnp.exp2(s - m).astype(BF16)
        v = qkv_ref[0, pl.ds(krow0, nk),
                    Q_WIDTH + KV_WIDTH + g * HEAD_DIM:Q_WIDTH + KV_WIDTH + (g + 1) * HEAD_DIM]
        lhs = jnp.concatenate([v.T, ones_rows], axis=0)
        res = jnp.dot(lhs, p, preferred_element_type=F32)
        l = res[HEAD_DIM:HEAD_DIM + 1]
        if has_sink:
            l = l + jnp.exp2(sink_ref[:, g * nl:(g + 1) * nl] - m)
        o_t = res[:HEAD_DIM] * (1.0 / l)
        for hp in range(GQA // 2):
            pair = jnp.concatenate([o_t[:, (2 * hp) * qb:(2 * hp + 1) * qb],
                                    o_t[:, (2 * hp + 1) * qb:(2 * hp + 2) * qb]], axis=0)
            c0 = (GQA * g + 2 * hp) * HEAD_DIM
            o_ref[0, pl.ds(row0, qb), c0:c0 + 2 * HEAD_DIM] = pair.T.astype(BF16)
        if lse_ref is None:
            return []
        lse = (m + jnp.log2(l)) * (1.0 / LOG2E)
        return [lse[:, hh * qb:(hh + 1) * qb] for hh in range(GQA)]

    def unit_mask(off):
        table = sum((off > o).astype(jnp.int32) for o in offs[:-1]) if len(offs) > 1 else 0
        valid = mask_ref[table] > 0.5
        return jnp.concatenate([valid] * GQA, axis=1)

    def step(t_cur, t_next, slot_cur, slot_next):
        nrow0, nkrow0, noff = unit_rows(t_next)
        nvalid4 = unit_mask(noff)
        if t_cur is None:
            for g in range(N_KV_HEADS):
                s, m = scores(nrow0, nkrow0, nvalid4, g)
                s_scr[slot_next, g] = s
                m_scr[slot_next, g] = m
            return
        row0, krow0, _ = unit_rows(t_cur)
        lse_rows = []
        for g in range(N_KV_HEADS):
            s, m = scores(nrow0, nkrow0, nvalid4, g)
            s_scr[slot_next, g] = s
            m_scr[slot_next, g] = m
            lse_rows += finish(row0, krow0, g, s_scr[slot_cur, g], m_scr[slot_cur, g])
        if lse_ref is not None:
            pad = jnp.zeros((LANES - N_HEADS, qb), F32)
            lse_ref[0, pl.ds(row0, qb), :] = jnp.concatenate(lse_rows + [pad], axis=0).T

    step(None, 0, None, 0)

    def body(tt, carry):
        t = units_per_trip * tt
        for u in range(units_per_trip):
            step(t + u, jnp.minimum(t + u + 1, n_units - 1), u % 2, (u + 1) % 2)
        return carry

    lax.fori_loop(0, n_units // units_per_trip, body, 0)


def _attn(qkv, d, w, sink_rows=None):
    b, seq, _ = qkv.shape
    has_sink = sink_rows is not None
    length = seq // d
    nk = min(Q_BLOCK + 2 * w, length)
    masks, offs = _mask_tables(length, Q_BLOCK, w, nk)
    units_per_trip = MAX_UNITS_PER_TRIP if length > Q_BLOCK else MAX_UNITS_PER_TRIP // 2
    kern = functools.partial(_attn_kernel, seq=seq, d=d, w=w, offs=tuple(offs),
                             units_per_trip=units_per_trip, has_sink=has_sink)
    in_specs = [pl.BlockSpec(masks.shape, lambda i: (0, 0, 0)),
                pl.BlockSpec((1, seq, QKV_WIDTH), lambda i: (i, 0, 0))]
    args = [masks, qkv]
    o_spec = pl.BlockSpec((1, seq, Q_WIDTH), lambda i: (i, 0, 0))
    o_shape = jax.ShapeDtypeStruct((b, seq, Q_WIDTH), BF16)
    if has_sink:
        in_specs = [pl.BlockSpec((1, N_HEADS * Q_BLOCK), lambda i: (0, 0))] + in_specs
        args = [sink_rows] + args
        out_specs, out_shape = o_spec, o_shape
    else:
        out_specs = (o_spec, pl.BlockSpec((1, seq, LANES), lambda i: (i, 0, 0)))
        out_shape = (o_shape, jax.ShapeDtypeStruct((b, seq, LANES), F32))
    return pl.pallas_call(
        kern,
        grid=(b,),
        in_specs=in_specs,
        out_specs=out_specs,
        out_shape=out_shape,
        scratch_shapes=[pltpu.VMEM((2, N_KV_HEADS, nk, GQA * Q_BLOCK), F32),
                        pltpu.VMEM((2, N_KV_HEADS, 1, GQA * Q_BLOCK), F32)],
        compiler_params=pltpu.CompilerParams(
            dimension_semantics=("arbitrary",), vmem_limit_bytes=VMEM_LIMIT),
        name=f"band_attn_d{d}",
    )(*args)


PERM_ROWS = 256


def _mix_kernel(*refs, dils):
    n = len(dils)
    o_refs, lse_refs = refs[:n], refs[n:2 * n]
    perm_refs = refs[2 * n:3 * n]
    e_ref, out_ref = refs[3 * n:]
    tm = out_ref.shape[0]

    def token_order(ref, d):
        a = ref[0]
        if d > 1:
            a = jnp.swapaxes(a, 0, 1)
        return a.reshape(tm, a.shape[-1])

    lses = [token_order(r, d) for r, d in zip(lse_refs, dils)]
    mx = functools.reduce(jnp.maximum, lses)
    es = [jnp.exp(l - mx) for l in lses]
    inv = 1.0 / functools.reduce(jnp.add, es)
    wfulls = []
    for e in es[1:]:
        wt = e * inv
        hi = wt.astype(BF16)
        lo = (wt - hi.astype(F32)).astype(BF16)
        wfulls.append(jnp.dot(jnp.concatenate([hi, lo], axis=1), e_ref[...],
                              preferred_element_type=F32))
    for j in range(tm // PERM_ROWS):
        rows = slice(j * PERM_ROWS, (j + 1) * PERM_ROWS)
        toks = []
        for o_ref, p_ref, d in zip(o_refs, perm_refs, dils):
            blk = o_ref[0, :, j * (PERM_ROWS // d):(j + 1) * (PERM_ROWS // d), :]
            blk = blk.reshape(PERM_ROWS, Q_WIDTH)
            if d > 1:
                toks.append(jnp.dot(p_ref[...], blk, preferred_element_type=F32))
            else:
                toks.append(blk.astype(F32))
        acc = toks[0]
        for wfull, tok in zip(wfulls, toks[1:]):
            acc = acc + wfull[rows, :] * (tok - toks[0])
        out_ref[rows, :] = acc.astype(BF16)


def _mix(os_, lses, dils, batch, seq, tm=1024):
    n_seq_blocks = seq // tm
    head_of_lane = np.arange(Q_WIDTH) // HEAD_DIM
    spread = (np.arange(LANES)[:, None] == head_of_lane[None, :]).astype(np.float32)
    spread = jnp.asarray(np.concatenate([spread, spread], axis=0), BF16)
    perms = []
    for d in dils:
        n_tok = np.arange(PERM_ROWS)
        src_row = (n_tok % d) * (PERM_ROWS // d) + n_tok // d
        perms.append(jnp.asarray(np.arange(PERM_ROWS)[None, :] == src_row[:, None], BF16))
    idx = lambda i: (i // n_seq_blocks, 0, i % n_seq_blocks, 0)
    const = lambda i: (0, 0)
    in_specs, args = [], []
    for a, width in ((os_, Q_WIDTH), (lses, LANES)):
        for arr, d in zip(a, dils):
            in_specs.append(pl.BlockSpec((1, d, tm // d, width), idx))
            args.append(arr.reshape(batch, d, seq // d, width))
    in_specs += [pl.BlockSpec((PERM_ROWS, PERM_ROWS), const)] * len(dils)
    in_specs.append(pl.BlockSpec((2 * LANES, Q_WIDTH), const))
    return pl.pallas_call(
        functools.partial(_mix_kernel, dils=tuple(dils)),
        grid=(batch * n_seq_blocks,),
        in_specs=in_specs,
        out_specs=pl.BlockSpec((tm, Q_WIDTH), lambda i: (i, 0)),
        out_shape=jax.ShapeDtypeStruct((batch * seq, Q_WIDTH), BF16),
        compiler_params=pltpu.CompilerParams(
            dimension_semantics=("arbitrary",), vmem_limit_bytes=VMEM_LIMIT),
        name="group_mix",
    )(*args, *perms, spread)


def _post_kernel(*refs, final, ts):
    x_ref, o_ref, wo_ref, g_ref, wg_ref, wu_ref, wd_ref = refs[:7]
    gf_ref = refs[7] if final else None
    out_ref = refs[-1]
    tm = x_ref.shape[0]
    for sub in range(tm // ts):
        rows = slice(sub * ts, (sub + 1) * ts)
        x1 = x_ref[rows, :] + jnp.dot(o_ref[rows, :], wo_ref[...], preferred_element_type=F32)
        h = _rmsnorm(x1, g_ref[...]).astype(BF16)
        gate = jnp.dot(h, wg_ref[...], preferred_element_type=F32)
        up = jnp.dot(h, wu_ref[...], preferred_element_type=F32)
        act = (gate * jax.nn.sigmoid(gate) * up).astype(BF16)
        x2 = x1 + jnp.dot(act, wd_ref[...], preferred_element_type=F32)
        if final:
            x2 = _rmsnorm(x2, gf_ref[...])
        out_ref[rows, :] = x2


def _post(x2d, o2d, wo, g, wg, wu, wd, gf=None, tm=1024, ts=512):
    t = x2d.shape[0]
    final = gf is not None
    d_ff = wg.shape[1]
    tile = lambda i: (i, 0)
    const = lambda i: (0, 0)
    resident = functools.partial(pl.BlockSpec, index_map=const, pipeline_mode=pl.Buffered(1))
    in_specs = [pl.BlockSpec((tm, D_MODEL), tile), pl.BlockSpec((tm, Q_WIDTH), tile),
                resident((Q_WIDTH, D_MODEL)), resident((1, D_MODEL)),
                resident((D_MODEL, d_ff)), resident((D_MODEL, d_ff)), resident((d_ff, D_MODEL))]
    args = [x2d, o2d, wo, g, wg, wu, wd]
    if final:
        in_specs.append(resident((1, D_MODEL)))
        args.append(gf)
    return pl.pallas_call(
        functools.partial(_post_kernel, final=final, ts=ts),
        grid=(t // tm,),
        in_specs=in_specs,
        out_specs=pl.BlockSpec((tm, D_MODEL), tile),
        out_shape=jax.ShapeDtypeStruct((t, D_MODEL), F32),
        compiler_params=pltpu.CompilerParams(
            dimension_semantics=("arbitrary",), vmem_limit_bytes=VMEM_LIMIT),
        name="out_ffn_final" if final else "out_ffn",
    )(*args)


def _rope_tables(seq):
    f32 = np.float32
    inv_freq = (f32(1.0) / (f32(ROPE_THETA) ** (np.arange(0, HEAD_DIM, 2, dtype=f32) / f32(HEAD_DIM))))
    ang = np.arange(seq, dtype=f32)[:, None] * inv_freq[None, :].astype(f32)
    cos = np.cos(ang).astype(f32)
    sin = np.sin(ang).astype(f32)
    cos_t = np.tile(cos, (1, LANES // (HEAD_DIM // 2)))
    sin_t = np.tile(np.concatenate([-sin, sin], axis=1), (1, LANES // HEAD_DIM))
    return cos_t, sin_t


def kernel(x, a_w_in, a_sink, a_w_out, b_w_in, b_w_out, norm_mix, norm_ffn,
           w_gate, w_up, w_down, final_norm):
    b, s, dm = x.shape
    t = b * s
    cos_t, sin_t = _rope_tables(s)
    row = lambda v: v.reshape(1, -1).astype(F32)
    bf = lambda a: a.astype(BF16)

    x2d = x.reshape(t, dm)
    (qkv,) = _proj(x2d, row(norm_mix[0]), bf(a_w_in[0]), cos_t, sin_t, b, s, (1,))
    sink_rows = jnp.repeat(a_sink[0].astype(F32) * LOG2E, Q_BLOCK).reshape(1, N_HEADS * Q_BLOCK)
    o = _attn(qkv, 1, ATTN_HALF_WINDOW, sink_rows=sink_rows)
    x2d = _post(x2d, o.reshape(t, Q_WIDTH), bf(a_w_out[0]), row(norm_ffn[0]),
                bf(w_gate[0]), bf(w_up[0]), bf(w_down[0]))

    dils = tuple(dil for _, dil in DILATED_GROUPS)
    qkvs = _proj(x2d, row(norm_mix[1]), bf(b_w_in[0]), cos_t, sin_t, b, s, dils)
    os_, lses = [], []
    for qkv, (window, dil) in zip(qkvs, DILATED_GROUPS):
        o, lse = _attn(qkv, dil, window // 2 // dil)
        os_.append(o)
        lses.append(lse)
    o = _mix(os_, lses, dils, b, s)
    out = _post(x2d, o, bf(b_w_out[0]), row(norm_ffn[1]),
                bf(w_gate[1]), bf(w_up[1]), bf(w_down[1]), gf=row(final_norm))
    return out.reshape(b, s, dm)
```

```python
import functools
import math

import jax
import jax.numpy as jnp
import numpy as np
from jax import lax
from jax.experimental import pallas as pl
from jax.experimental.pallas import tpu as pltpu

D_MODEL = 1024
HEAD_DIM = 64
N_HEADS = 16
N_KV_HEADS = 4
GQA = N_HEADS // N_KV_HEADS
Q_WIDTH = N_HEADS * HEAD_DIM
KV_WIDTH = N_KV_HEADS * HEAD_DIM
QKV_WIDTH = Q_WIDTH + 2 * KV_WIDTH
ATTN_HALF_WINDOW = 128
DILATED_GROUPS = ((128, 1), (512, 4), (2048, 16))
ROPE_THETA = 10000.0
RMS_EPS = 1e-6
NEG_INF = -1e30
LANES = 128
BF16_SUBLANE_ROWS = 16
VMEM_LIMIT = 56 * 1024 * 1024
ROW_TILE = 1024
FFN_SUB_TILE = 512
PROJ_SUB_TILE = 256

LOG2E = math.log2(math.e)
Q_SCALE = LOG2E / math.sqrt(HEAD_DIM)

BF16 = jnp.bfloat16
F32 = jnp.float32


def _rmsnorm(x, g):
    ms = jnp.mean(x * x, axis=-1, keepdims=True)
    return x * lax.rsqrt(ms + RMS_EPS) * g


def _proj_kernel(*refs, dils, ts):
    n = len(dils)
    x_ref, g_ref, w_ref, cos_ref, sin_ref = refs[:5]
    o_refs = refs[5:5 + n]
    tm = x_ref.shape[0]
    lane = lax.broadcasted_iota(jnp.int32, (ts, LANES), 1)
    first_half = (lane % HEAD_DIM) < (HEAD_DIM // 2)
    n_rope = (Q_WIDTH + KV_WIDTH) // LANES
    n_q = Q_WIDTH // LANES
    for sub in range(tm // ts):
        rows = slice(sub * ts, (sub + 1) * ts)
        h = _rmsnorm(x_ref[rows, :], g_ref[...])
        for gi, (d, o_ref) in enumerate(zip(dils, o_refs)):
            hg = h
            if d > 1:
                hg = jnp.swapaxes(h.reshape(ts // d, d, D_MODEL), 0, 1).reshape(ts, D_MODEL)
            y = jnp.dot(hg.astype(BF16), w_ref[:, gi * QKV_WIDTH:(gi + 1) * QKV_WIDTH],
                        preferred_element_type=F32)
            cos = cos_ref[gi, rows, :]
            sin = sin_ref[gi, rows, :]
            for c in range(QKV_WIDTH // LANES):
                t = y[:, c * LANES:(c + 1) * LANES]
                if c < n_rope:
                    swapped = jnp.where(first_half,
                                        pltpu.roll(t, LANES - HEAD_DIM // 2, 1),
                                        pltpu.roll(t, HEAD_DIM // 2, 1))
                    t = t * cos + swapped * sin
                    if c < n_q:
                        t = t * Q_SCALE
                o_ref[0, :, sub * (ts // d):(sub + 1) * (ts // d), c * LANES:(c + 1) * LANES] = (
                    t.astype(BF16).reshape(d, ts // d, LANES))


def _tile_residue_order(table, d, tm):
    s, n = table.shape
    return table.reshape(s // tm, tm // d, d, n).swapaxes(1, 2).reshape(s, n)


def _proj(x2d, g, w, cos_t, sin_t, batch, seq, dils, tm=ROW_TILE, ts=PROJ_SUB_TILE):
    n = len(dils)
    n_seq_blocks = seq // tm
    cos_g = np.stack([_tile_residue_order(cos_t, d, ts) for d in dils])
    sin_g = np.stack([_tile_residue_order(sin_t, d, ts) for d in dils])
    table_spec = pl.BlockSpec((n, tm, LANES), lambda i: (0, i % n_seq_blocks, 0))
    outs = pl.pallas_call(
        functools.partial(_proj_kernel, dils=tuple(dils), ts=ts),
        grid=(batch * n_seq_blocks,),
        in_specs=[
            pl.BlockSpec((tm, D_MODEL), lambda i: (i, 0)),
            pl.BlockSpec((1, D_MODEL), lambda i: (0, 0)),
            pl.BlockSpec((D_MODEL, n * QKV_WIDTH), lambda i: (0, 0), pipeline_mode=pl.Buffered(1)),
            table_spec, table_spec,
        ],
        out_specs=[pl.BlockSpec((1, d, tm // d, QKV_WIDTH),
                                lambda i: (i // n_seq_blocks, 0, i % n_seq_blocks, 0)) for d in dils],
        out_shape=[jax.ShapeDtypeStruct((batch, d, seq // d, QKV_WIDTH), BF16) for d in dils],
        compiler_params=pltpu.CompilerParams(
            dimension_semantics=("arbitrary",), vmem_limit_bytes=VMEM_LIMIT),
        name=f"qkv_proj_x{n}",
    )(x2d, g, w, cos_g, sin_g)
    return [o.reshape(batch, seq, QKV_WIDTH) for o in outs]


Q_BLOCK = 128
MAX_UNITS_PER_TRIP = 8


def _unit_key_start(i, qb, w, length, nk):
    return min(max(i * qb - w, 0), length - nk)


def _mask_tables(length, qb, w, nk):
    nqb = length // qb
    offs = sorted({_unit_key_start(i, qb, w, length, nk) - i * qb for i in range(nqb)})
    kpos = np.arange(nk)[:, None]
    qpos = np.arange(qb)[None, :]
    tabs = [(np.abs(kpos + off - qpos) <= w).astype(np.float32) for off in offs]
    return np.stack(tabs), offs


def _attn_kernel(*refs, seq, d, w, offs, units_per_trip, has_sink):
    if has_sink:
        sink_ref, mask_ref, qkv_ref, o_ref, s_scr, m_scr = refs
        lse_ref = None
    else:
        mask_ref, qkv_ref, o_ref, lse_ref, s_scr, m_scr = refs
    qb = Q_BLOCK
    length = seq // d
    nqb = length // qb
    nk = min(qb + 2 * w, length)
    n_units = d * nqb
    nl = GQA * qb
    ones_rows = jnp.ones((BF16_SUBLANE_ROWS, nk), BF16)

    def unit_rows(t):
        r = t // nqb
        q0 = (t % nqb) * qb
        ks = jnp.clip(q0 - w, 0, length - nk)
        row0 = pl.multiple_of(r * length + q0, qb)
        krow0 = pl.multiple_of(r * length + ks, BF16_SUBLANE_ROWS)
        return row0, krow0, ks - q0

    def scores(row0, krow0, valid4, g):
        k = qkv_ref[0, pl.ds(krow0, nk), Q_WIDTH + g * HEAD_DIM:Q_WIDTH + (g + 1) * HEAD_DIM]
        q4 = jnp.concatenate(
            [qkv_ref[0, pl.ds(row0, qb), (GQA * g + hh) * HEAD_DIM:(GQA * g + hh + 1) * HEAD_DIM]
             for hh in range(GQA)], axis=0)
        s = lax.dot_general(k, q4, (((1,), (1,)), ((), ())), preferred_element_type=F32)
        s = jnp.where(valid4, s, NEG_INF)
        m = jnp.max(s, axis=0, keepdims=True)
        if has_sink:
            m = jnp.maximum(m, sink_ref[:, g * nl:(g + 1) * nl])
        return s, m

    def finish(row0, krow0, g, s, m):
        p = jnp.exp2(s - m).astype(BF16)
        v = qkv_ref[0, pl.ds(krow0, nk),
                    Q_WIDTH + KV_WIDTH + g * HEAD_DIM:Q_WIDTH + KV_WIDTH + (g + 1) * HEAD_DIM]
        lhs = jnp.concatenate([v.T, ones_rows], axis=0)
        res = jnp.dot(lhs, p, preferred_element_type=F32)
        l = res[HEAD_DIM:HEAD_DIM + 1]
        if has_sink:
            l = l + jnp.exp2(sink_ref[:, g * nl:(g + 1) * nl] - m)
        o_t = res[:HEAD_DIM] * (1.0 / l)
        for hp in range(GQA // 2):
            pair = jnp.concatenate([o_t[:, (2 * hp) * qb:(2 * hp + 1) * qb],
                                    o_t[:, (2 * hp + 1) * qb:(2 * hp + 2) * qb]], axis=0)
            c0 = (GQA * g + 2 * hp) * HEAD_DIM
            o_ref[0, pl.ds(row0, qb), c0:c0 + 2 * HEAD_DIM] = pair.T.astype(BF16)
        if lse_ref is None:
            return []
        lse = (m + jnp.log2(l)) * (1.0 / LOG2E)
        return [lse[:, hh * qb:(hh + 1) * qb] for hh in range(GQA)]

    def unit_mask(off):
        table = sum((off > o).astype(jnp.int32) for o in offs[:-1]) if len(offs) > 1 else 0
        valid = mask_ref[table] > 0.5
        return jnp.concatenate([valid] * GQA, axis=1)

    def step(t_cur, t_next, slot_cur, slot_next):
        nrow0, nkrow0, noff = unit_rows(t_next)
        nvalid4 = unit_mask(noff)
        if t_cur is None:
            for g in range(N_KV_HEADS):
                s, m = scores(nrow0, nkrow0, nvalid4, g)
                s_scr[slot_next, g] = s
                m_scr[slot_next, g] = m
            return
        row0, krow0, _ = unit_rows(t_cur)
        lse_rows = []
        for g in range(N_KV_HEADS):
            s, m = scores(nrow0, nkrow0, nvalid4, g)
            s_scr[slot_next, g] = s
            m_scr[slot_next, g] = m
            lse_rows += finish(row0, krow0, g, s_scr[slot_cur, g], m_scr[slot_cur, g])
        if lse_ref is not None:
            pad = jnp.zeros((LANES - N_HEADS, qb), F32)
            lse_ref[0, pl.ds(row0, qb), :] = jnp.concatenate(lse_rows + [pad], axis=0).T

    step(None, 0, None, 0)

    def body(tt, carry):
        t = units_per_trip * tt
        for u in range(units_per_trip):
            step(t + u, jnp.minimum(t + u + 1, n_units - 1), u % 2, (u + 1) % 2)
        return carry

    lax.fori_loop(0, n_units // units_per_trip, body, 0)


def _attn(qkv, d, w, sink_rows=None):
    b, seq, _ = qkv.shape
    has_sink = sink_rows is not None
    length = seq // d
    nk = min(Q_BLOCK + 2 * w, length)
    masks, offs = _mask_tables(length, Q_BLOCK, w, nk)
    units_per_trip = MAX_UNITS_PER_TRIP if length > Q_BLOCK else MAX_UNITS_PER_TRIP // 2
    kern = functools.partial(_attn_kernel, seq=seq, d=d, w=w, offs=tuple(offs),
                             units_per_trip=units_per_trip, has_sink=has_sink)
    in_specs = [pl.BlockSpec(masks.shape, lambda i: (0, 0, 0)),
                pl.BlockSpec((1, seq, QKV_WIDTH), lambda i: (i, 0, 0))]
    args = [masks, qkv]
    o_spec = pl.BlockSpec((1, seq, Q_WIDTH), lambda i: (i, 0, 0))
    o_shape = jax.ShapeDtypeStruct((b, seq, Q_WIDTH), BF16)
    if has_sink:
        in_specs = [pl.BlockSpec((1, N_HEADS * Q_BLOCK), lambda i: (0, 0))] + in_specs
        args = [sink_rows] + args
        out_specs, out_shape = o_spec, o_shape
    else:
        out_specs = (o_spec, pl.BlockSpec((1, seq, LANES), lambda i: (i, 0, 0)))
        out_shape = (o_shape, jax.ShapeDtypeStruct((b, seq, LANES), F32))
    return pl.pallas_call(
        kern,
        grid=(b,),
        in_specs=in_specs,
        out_specs=out_specs,
        out_shape=out_shape,
        scratch_shapes=[pltpu.VMEM((2, N_KV_HEADS, nk, GQA * Q_BLOCK), F32),
                        pltpu.VMEM((2, N_KV_HEADS, 1, GQA * Q_BLOCK), F32)],
        compiler_params=pltpu.CompilerParams(
            dimension_semantics=("arbitrary",), vmem_limit_bytes=VMEM_LIMIT),
        name=f"band_attn_d{d}",
    )(*args)


PERM_ROWS = 256


def _mix_kernel(*refs, dils):
    n = len(dils)
    o_refs, lse_refs = refs[:n], refs[n:2 * n]
    perm_refs = refs[2 * n:3 * n]
    e_ref, out_ref = refs[3 * n:]
    tm = out_ref.shape[0]

    def token_order(ref, d):
        a = ref[0]
        if d > 1:
            a = jnp.swapaxes(a, 0, 1)
        return a.reshape(tm, a.shape[-1])

    lses = [token_order(r, d) for r, d in zip(lse_refs, dils)]
    mx = functools.reduce(jnp.maximum, lses)
    es = [jnp.exp(l - mx) for l in lses]
    inv = 1.0 / functools.reduce(jnp.add, es)
    wfulls = []
    for e in es[1:]:
        wt = e * inv
        hi = wt.astype(BF16)
        lo = (wt - hi.astype(F32)).astype(BF16)
        wfulls.append(jnp.dot(jnp.concatenate([hi, lo], axis=1), e_ref[...],
                              preferred_element_type=F32))
    for j in range(tm // PERM_ROWS):
        rows = slice(j * PERM_ROWS, (j + 1) * PERM_ROWS)
        toks = []
        for o_ref, p_ref, d in zip(o_refs, perm_refs, dils):
            blk = o_ref[0, :, j * (PERM_ROWS // d):(j + 1) * (PERM_ROWS // d), :]
            blk = blk.reshape(PERM_ROWS, Q_WIDTH)
            if d > 1:
                toks.append(jnp.dot(p_ref[...], blk, preferred_element_type=F32))
            else:
                toks.append(blk.astype(F32))
        acc = toks[0]
        for wfull, tok in zip(wfulls, toks[1:]):
            acc = acc + wfull[rows, :] * (tok - toks[0])
        out_ref[rows, :] = acc.astype(BF16)


def _mix(os_, lses, dils, batch, seq, tm=ROW_TILE):
    n_seq_blocks = seq // tm
    head_of_lane = np.arange(Q_WIDTH) // HEAD_DIM
    spread = (np.arange(LANES)[:, None] == head_of_lane[None, :]).astype(np.float32)
    spread = jnp.asarray(np.concatenate([spread, spread], axis=0), BF16)
    perms = []
    for d in dils:
        n_tok = np.arange(PERM_ROWS)
        src_row = (n_tok % d) * (PERM_ROWS // d) + n_tok // d
        perms.append(jnp.asarray(np.arange(PERM_ROWS)[None, :] == src_row[:, None], BF16))
    idx = lambda i: (i // n_seq_blocks, 0, i % n_seq_blocks, 0)
    const = lambda i: (0, 0)
    in_specs, args = [], []
    for a, width in ((os_, Q_WIDTH), (lses, LANES)):
        for arr, d in zip(a, dils):
            in_specs.append(pl.BlockSpec((1, d, tm // d, width), idx))
            args.append(arr.reshape(batch, d, seq // d, width))
    in_specs += [pl.BlockSpec((PERM_ROWS, PERM_ROWS), const)] * len(dils)
    in_specs.append(pl.BlockSpec((2 * LANES, Q_WIDTH), const))
    return pl.pallas_call(
        functools.partial(_mix_kernel, dils=tuple(dils)),
        grid=(batch * n_seq_blocks,),
        in_specs=in_specs,
        out_specs=pl.BlockSpec((tm, Q_WIDTH), lambda i: (i, 0)),
        out_shape=jax.ShapeDtypeStruct((batch * seq, Q_WIDTH), BF16),
        compiler_params=pltpu.CompilerParams(
            dimension_semantics=("arbitrary",), vmem_limit_bytes=VMEM_LIMIT),
        name="group_mix",
    )(*args, *perms, spread)


def _post_kernel(*refs, final, ts):
    x_ref, o_ref, wo_ref, g_ref, wg_ref, wu_ref, wd_ref = refs[:7]
    gf_ref = refs[7] if final else None
    out_ref = refs[-1]
    tm = x_ref.shape[0]
    for sub in range(tm // ts):
        rows = slice(sub * ts, (sub + 1) * ts)
        x1 = x_ref[rows, :] + jnp.dot(o_ref[rows, :], wo_ref[...], preferred_element_type=F32)
        h = _rmsnorm(x1, g_ref[...]).astype(BF16)
        gate = jnp.dot(h, wg_ref[...], preferred_element_type=F32)
        up = jnp.dot(h, wu_ref[...], preferred_element_type=F32)
        act = (gate * jax.nn.sigmoid(gate) * up).astype(BF16)
        x2 = x1 + jnp.dot(act, wd_ref[...], preferred_element_type=F32)
        if final:
            x2 = _rmsnorm(x2, gf_ref[...])
        out_ref[rows, :] = x2


def _post(x2d, o2d, wo, g, wg, wu, wd, gf=None, tm=ROW_TILE, ts=FFN_SUB_TILE):
    t = x2d.shape[0]
    final = gf is not None
    d_ff = wg.shape[1]
    tile = lambda i: (i, 0)
    const = lambda i: (0, 0)
    resident = functools.partial(pl.BlockSpec, index_map=const, pipeline_mode=pl.Buffered(1))
    in_specs = [pl.BlockSpec((tm, D_MODEL), tile), pl.BlockSpec((tm, Q_WIDTH), tile),
                resident((Q_WIDTH, D_MODEL)), resident((1, D_MODEL)),
                resident((D_MODEL, d_ff)), resident((D_MODEL, d_ff)), resident((d_ff, D_MODEL))]
    args = [x2d, o2d, wo, g, wg, wu, wd]
    if final:
        in_specs.append(resident((1, D_MODEL)))
        args.append(gf)
    return pl.pallas_call(
        functools.partial(_post_kernel, final=final, ts=ts),
        grid=(t // tm,),
        in_specs=in_specs,
        out_specs=pl.BlockSpec((tm, D_MODEL), tile),
        out_shape=jax.ShapeDtypeStruct((t, D_MODEL), F32),
        compiler_params=pltpu.CompilerParams(
            dimension_semantics=("arbitrary",), vmem_limit_bytes=VMEM_LIMIT),
        name="out_ffn_final" if final else "out_ffn",
    )(*args)


def _rope_tables(seq):
    f32 = np.float32
    inv_freq = (f32(1.0) / (f32(ROPE_THETA) ** (np.arange(0, HEAD_DIM, 2, dtype=f32) / f32(HEAD_DIM))))
    ang = np.arange(seq, dtype=f32)[:, None] * inv_freq[None, :].astype(f32)
    cos = np.cos(ang).astype(f32)
    sin = np.sin(ang).astype(f32)
    cos_t = np.tile(cos, (1, LANES // (HEAD_DIM // 2)))
    sin_t = np.tile(np.concatenate([-sin, sin], axis=1), (1, LANES // HEAD_DIM))
    return cos_t, sin_t


def kernel(x, a_w_in, a_sink, a_w_out, b_w_in, b_w_out, norm_mix, norm_ffn,
           w_gate, w_up, w_down, final_norm):
    b, s, dm = x.shape
    t = b * s
    cos_t, sin_t = _rope_tables(s)
    row = lambda v: v.reshape(1, -1).astype(F32)
    bf = lambda a: a.astype(BF16)

    x2d = x.reshape(t, dm)
    (qkv,) = _proj(x2d, row(norm_mix[0]), bf(a_w_in[0]), cos_t, sin_t, b, s, (1,))
    sink_rows = jnp.repeat(a_sink[0].astype(F32) * LOG2E, Q_BLOCK).reshape(1, N_HEADS * Q_BLOCK)
    o = _attn(qkv, 1, ATTN_HALF_WINDOW, sink_rows=sink_rows)
    x2d = _post(x2d, o.reshape(t, Q_WIDTH), bf(a_w_out[0]), row(norm_ffn[0]),
                bf(w_gate[0]), bf(w_up[0]), bf(w_down[0]))

    dils = tuple(dil for _, dil in DILATED_GROUPS)
    qkvs = _proj(x2d, row(norm_mix[1]), bf(b_w_in[0]), cos_t, sin_t, b, s, dils)
    os_, lses = [], []
    for qkv, (window, dil) in zip(qkvs, DILATED_GROUPS):
        o, lse = _attn(qkv, dil, window // 2 // dil)
        os_.append(o)
        lses.append(lse)
    o = _mix(os_, lses, dils, b, s)
    out = _post(x2d, o, bf(b_w_out[0]), row(norm_ffn[1]),
                bf(w_gate[1]), bf(w_up[1]), bf(w_down[1]), gf=row(final_norm))
    return out.reshape(b, s, dm)
```

```python
import functools
import math

import jax
import jax.numpy as jnp
import numpy as np
from jax import lax
from jax.experimental import pallas as pl
from jax.experimental.pallas import tpu as pltpu

D_MODEL = 1024
HEAD_DIM = 64
N_HEADS = 16
N_KV_HEADS = 4
GQA = N_HEADS // N_KV_HEADS
Q_WIDTH = N_HEADS * HEAD_DIM
KV_WIDTH = N_KV_HEADS * HEAD_DIM
QKV_WIDTH = Q_WIDTH + 2 * KV_WIDTH
ATTN_HALF_WINDOW = 128
DILATED_GROUPS = ((128, 1), (512, 4), (2048, 16))
ROPE_THETA = 10000.0
RMS_EPS = 1e-6
NEG_INF = -1e30
LANES = 128
BF16_SUBLANE_ROWS = 16
VMEM_LIMIT = 56 * 1024 * 1024
ROW_TILE = 1024
FFN_SUB_TILE = 512
PROJ_SUB_TILE = 256

LOG2E = math.log2(math.e)
Q_SCALE = LOG2E / math.sqrt(HEAD_DIM)

BF16 = jnp.bfloat16
F32 = jnp.float32


def _rmsnorm(x, g):
    ms = jnp.mean(x * x, axis=-1, keepdims=True)
    return x * lax.rsqrt(ms + RMS_EPS) * g


def _proj_kernel(*refs, dils, ts):
    n = len(dils)
    x_ref, g_ref, w_ref, cos_ref, sin_ref = refs[:5]
    o_refs = refs[5:5 + n]
    tm = x_ref.shape[0]
    lane = lax.broadcasted_iota(jnp.int32, (ts, LANES), 1)
    first_half = (lane % HEAD_DIM) < (HEAD_DIM // 2)
    n_rope = (Q_WIDTH + KV_WIDTH) // LANES
    n_q = Q_WIDTH // LANES
    for sub in range(tm // ts):
        rows = slice(sub * ts, (sub + 1) * ts)
        h = _rmsnorm(x_ref[rows, :], g_ref[...])
        for gi, (d, o_ref) in enumerate(zip(dils, o_refs)):
            hg = h
            if d > 1:
                hg = jnp.swapaxes(h.reshape(ts // d, d, D_MODEL), 0, 1).reshape(ts, D_MODEL)
            y = jnp.dot(hg.astype(BF16), w_ref[:, gi * QKV_WIDTH:(gi + 1) * QKV_WIDTH],
                        preferred_element_type=F32)
            cos = cos_ref[gi, rows, :]
            sin = sin_ref[gi, rows, :]
            for c in range(QKV_WIDTH // LANES):
                t = y[:, c * LANES:(c + 1) * LANES]
                if c < n_rope:
                    swapped = jnp.where(first_half,
                                        pltpu.roll(t, LANES - HEAD_DIM // 2, 1),
                                        pltpu.roll(t, HEAD_DIM // 2, 1))
                    t = t * cos + swapped * sin
                    if c < n_q:
                        t = t * Q_SCALE
                o_ref[0, :, sub * (ts // d):(sub + 1) * (ts // d), c * LANES:(c + 1) * LANES] = (
                    t.astype(BF16).reshape(d, ts // d, LANES))


def _tile_residue_order(table, d, tm):
    s, n = table.shape
    return table.reshape(s // tm, tm // d, d, n).swapaxes(1, 2).reshape(s, n)


def _proj(x2d, g, w, cos_t, sin_t, batch, seq, dils, tm=ROW_TILE, ts=PROJ_SUB_TILE):
    n = len(dils)
    n_seq_blocks = seq // tm
    assert x2d.shape == (batch * seq, D_MODEL) and w.shape == (D_MODEL, n * QKV_WIDTH)
    assert seq % tm == 0 and tm % ts == 0
    assert all(ts % d == 0 and (ts // d) % BF16_SUBLANE_ROWS == 0 for d in dils)
    cos_g = np.stack([_tile_residue_order(cos_t, d, ts) for d in dils])
    sin_g = np.stack([_tile_residue_order(sin_t, d, ts) for d in dils])
    table_spec = pl.BlockSpec((n, tm, LANES), lambda i: (0, i % n_seq_blocks, 0))
    outs = pl.pallas_call(
        functools.partial(_proj_kernel, dils=tuple(dils), ts=ts),
        grid=(batch * n_seq_blocks,),
        in_specs=[
            pl.BlockSpec((tm, D_MODEL), lambda i: (i, 0)),
            pl.BlockSpec((1, D_MODEL), lambda i: (0, 0)),
            pl.BlockSpec((D_MODEL, n * QKV_WIDTH), lambda i: (0, 0), pipeline_mode=pl.Buffered(1)),
            table_spec, table_spec,
        ],
        out_specs=[pl.BlockSpec((1, d, tm // d, QKV_WIDTH),
                                lambda i: (i // n_seq_blocks, 0, i % n_seq_blocks, 0)) for d in dils],
        out_shape=[jax.ShapeDtypeStruct((batch, d, seq // d, QKV_WIDTH), BF16) for d in dils],
        compiler_params=pltpu.CompilerParams(
            dimension_semantics=("arbitrary",), vmem_limit_bytes=VMEM_LIMIT),
        name=f"qkv_proj_x{n}",
    )(x2d, g, w, cos_g, sin_g)
    return [o.reshape(batch, seq, QKV_WIDTH) for o in outs]


Q_BLOCK = 128
MAX_UNITS_PER_TRIP = 8


def _unit_key_start(i, qb, w, length, nk):
    return min(max(i * qb - w, 0), length - nk)


def _mask_tables(length, qb, w, nk):
    nqb = length // qb
    offs = sorted({_unit_key_start(i, qb, w, length, nk) - i * qb for i in range(nqb)})
    kpos = np.arange(nk)[:, None]
    qpos = np.arange(qb)[None, :]
    tabs = [(np.abs(kpos + off - qpos) <= w).astype(np.float32) for off in offs]
    return np.stack(tabs), offs


def _attn_kernel(*refs, seq, d, w, offs, units_per_trip, has_sink):
    if has_sink:
        sink_ref, mask_ref, qkv_ref, o_ref, s_scr, m_scr = refs
        lse_ref = None
    else:
        mask_ref, qkv_ref, o_ref, lse_ref, s_scr, m_scr = refs
    qb = Q_BLOCK
    length = seq // d
    nqb = length // qb
    nk = min(qb + 2 * w, length)
    n_units = d * nqb
    nl = GQA * qb
    ones_rows = jnp.ones((BF16_SUBLANE_ROWS, nk), BF16)

    def unit_rows(t):
        r = t // nqb
        q0 = (t % nqb) * qb
        ks = jnp.clip(q0 - w, 0, length - nk)
        row0 = pl.multiple_of(r * length + q0, qb)
        krow0 = pl.multiple_of(r * length + ks, BF16_SUBLANE_ROWS)
        return row0, krow0, ks - q0

    def scores(row0, krow0, valid4, g):
        k = qkv_ref[0, pl.ds(krow0, nk), Q_WIDTH + g * HEAD_DIM:Q_WIDTH + (g + 1) * HEAD_DIM]
        q4 = jnp.concatenate(
            [qkv_ref[0, pl.ds(row0, qb), (GQA * g + hh) * HEAD_DIM:(GQA * g + hh + 1) * HEAD_DIM]
             for hh in range(GQA)], axis=0)
        s = lax.dot_general(k, q4, (((1,), (1,)), ((), ())), preferred_element_type=F32)
        s = jnp.where(valid4, s, NEG_INF)
        m = jnp.max(s, axis=0, keepdims=True)
        if has_sink:
            m = jnp.maximum(m, sink_ref[:, g * nl:(g + 1) * nl])
        return s, m

    def finish(row0, krow0, g, s, m):
        p = jnp.exp2(s - m).astype(BF16)
        v = qkv_ref[0, pl.ds(krow0, nk),
                    Q_WIDTH + KV_WIDTH + g * HEAD_DIM:Q_WIDTH + KV_WIDTH + (g + 1) * HEAD_DIM]
        lhs = jnp.concatenate([v.T, ones_rows], axis=0)
        res = jnp.dot(lhs, p, preferred_element_type=F32)
        l = res[HEAD_DIM:HEAD_DIM + 1]
        if has_sink:
            l = l + jnp.exp2(sink_ref[:, g * nl:(g + 1) * nl] - m)
        o_t = res[:HEAD_DIM] * (1.0 / l)
        for hp in range(GQA // 2):
            pair = jnp.concatenate([o_t[:, (2 * hp) * qb:(2 * hp + 1) * qb],
                                    o_t[:, (2 * hp + 1) * qb:(2 * hp + 2) * qb]], axis=0)
            c0 = (GQA * g + 2 * hp) * HEAD_DIM
            o_ref[0, pl.ds(row0, qb), c0:c0 + 2 * HEAD_DIM] = pair.T.astype(BF16)
        if lse_ref is None:
            return []
        lse = (m + jnp.log2(l)) * (1.0 / LOG2E)
        return [lse[:, hh * qb:(hh + 1) * qb] for hh in range(GQA)]

    def unit_mask(off):
        table = sum((off > o).astype(jnp.int32) for o in offs[:-1]) if len(offs) > 1 else 0
        valid = mask_ref[table] > 0.5
        return jnp.concatenate([valid] * GQA, axis=1)

    def step(t_cur, t_next, slot_cur, slot_next):
        nrow0, nkrow0, noff = unit_rows(t_next)
        nvalid4 = unit_mask(noff)
        if t_cur is None:
            for g in range(N_KV_HEADS):
                s, m = scores(nrow0, nkrow0, nvalid4, g)
                s_scr[slot_next, g] = s
                m_scr[slot_next, g] = m
            return
        row0, krow0, _ = unit_rows(t_cur)
        lse_rows = []
        for g in range(N_KV_HEADS):
            s, m = scores(nrow0, nkrow0, nvalid4, g)
            s_scr[slot_next, g] = s
            m_scr[slot_next, g] = m
            lse_rows += finish(row0, krow0, g, s_scr[slot_cur, g], m_scr[slot_cur, g])
        if lse_ref is not None:
            pad = jnp.zeros((LANES - N_HEADS, qb), F32)
            lse_ref[0, pl.ds(row0, qb), :] = jnp.concatenate(lse_rows + [pad], axis=0).T

    step(None, 0, None, 0)

    def body(tt, carry):
        t = units_per_trip * tt
        for u in range(units_per_trip):
            step(t + u, jnp.minimum(t + u + 1, n_units - 1), u % 2, (u + 1) % 2)
        return carry

    lax.fori_loop(0, n_units // units_per_trip, body, 0)


def _attn(qkv, d, w, sink_rows=None):
    b, seq, _ = qkv.shape
    has_sink = sink_rows is not None
    length = seq // d
    nk = min(Q_BLOCK + 2 * w, length)
    masks, offs = _mask_tables(length, Q_BLOCK, w, nk)
    units_per_trip = MAX_UNITS_PER_TRIP if length > Q_BLOCK else MAX_UNITS_PER_TRIP // 2
    assert qkv.shape[2] == QKV_WIDTH and seq % d == 0 and length % Q_BLOCK == 0
    assert w % BF16_SUBLANE_ROWS == 0 and (length - nk) % BF16_SUBLANE_ROWS == 0
    assert units_per_trip % 2 == 0 and (seq // Q_BLOCK) % units_per_trip == 0
    kern = functools.partial(_attn_kernel, seq=seq, d=d, w=w, offs=tuple(offs),
                             units_per_trip=units_per_trip, has_sink=has_sink)
    in_specs = [pl.BlockSpec(masks.shape, lambda i: (0, 0, 0)),
                pl.BlockSpec((1, seq, QKV_WIDTH), lambda i: (i, 0, 0))]
    args = [masks, qkv]
    o_spec = pl.BlockSpec((1, seq, Q_WIDTH), lambda i: (i, 0, 0))
    o_shape = jax.ShapeDtypeStruct((b, seq, Q_WIDTH), BF16)
    if has_sink:
        in_specs = [pl.BlockSpec((1, N_HEADS * Q_BLOCK), lambda i: (0, 0))] + in_specs
        args = [sink_rows] + args
        out_specs, out_shape = o_spec, o_shape
    else:
        out_specs = (o_spec, pl.BlockSpec((1, seq, LANES), lambda i: (i, 0, 0)))
        out_shape = (o_shape, jax.ShapeDtypeStruct((b, seq, LANES), F32))
    return pl.pallas_call(
        kern,
        grid=(b,),
        in_specs=in_specs,
        out_specs=out_specs,
        out_shape=out_shape,
        scratch_shapes=[pltpu.VMEM((2, N_KV_HEADS, nk, GQA * Q_BLOCK), F32),
                        pltpu.VMEM((2, N_KV_HEADS, 1, GQA * Q_BLOCK), F32)],
        compiler_params=pltpu.CompilerParams(
            dimension_semantics=("arbitrary",), vmem_limit_bytes=VMEM_LIMIT),
        name=f"band_attn_d{d}",
    )(*args)


PERM_ROWS = 256


def _mix_kernel(*refs, dils):
    n = len(dils)
    o_refs, lse_refs = refs[:n], refs[n:2 * n]
    perm_refs = refs[2 * n:3 * n]
    e_ref, out_ref = refs[3 * n:]
    tm = out_ref.shape[0]

    def token_order(ref, d):
        a = ref[0]
        if d > 1:
            a = jnp.swapaxes(a, 0, 1)
        return a.reshape(tm, a.shape[-1])

    lses = [token_order(r, d) for r, d in zip(lse_refs, dils)]
    mx = functools.reduce(jnp.maximum, lses)
    es = [jnp.exp(l - mx) for l in lses]
    inv = 1.0 / functools.reduce(jnp.add, es)
    wfulls = []
    for e in es[1:]:
        wt = e * inv
        hi = wt.astype(BF16)
        lo = (wt - hi.astype(F32)).astype(BF16)
        wfulls.append(jnp.dot(jnp.concatenate([hi, lo], axis=1), e_ref[...],
                              preferred_element_type=F32))
    for j in range(tm // PERM_ROWS):
        rows = slice(j * PERM_ROWS, (j + 1) * PERM_ROWS)
        toks = []
        for o_ref, p_ref, d in zip(o_refs, perm_refs, dils):
            blk = o_ref[0, :, j * (PERM_ROWS // d):(j + 1) * (PERM_ROWS // d), :]
            blk = blk.reshape(PERM_ROWS, Q_WIDTH)
            if d > 1:
                toks.append(jnp.dot(p_ref[...], blk, preferred_element_type=F32))
            else:
                toks.append(blk.astype(F32))
        acc = toks[0]
        for wfull, tok in zip(wfulls, toks[1:]):
            acc = acc + wfull[rows, :] * (tok - toks[0])
        out_ref[rows, :] = acc.astype(BF16)


def _mix(os_, lses, dils, batch, seq, tm=ROW_TILE):
    n_seq_blocks = seq // tm
    assert seq % tm == 0 and tm % PERM_ROWS == 0
    assert all(PERM_ROWS % d == 0 and (PERM_ROWS // d) % BF16_SUBLANE_ROWS == 0 for d in dils)
    assert len(os_) == len(lses) == len(dils)
    head_of_lane = np.arange(Q_WIDTH) // HEAD_DIM
    spread = (np.arange(LANES)[:, None] == head_of_lane[None, :]).astype(np.float32)
    spread = jnp.asarray(np.concatenate([spread, spread], axis=0), BF16)
    perms = []
    for d in dils:
        n_tok = np.arange(PERM_ROWS)
        src_row = (n_tok % d) * (PERM_ROWS // d) + n_tok // d
        perms.append(jnp.asarray(np.arange(PERM_ROWS)[None, :] == src_row[:, None], BF16))
    idx = lambda i: (i // n_seq_blocks, 0, i % n_seq_blocks, 0)
    const = lambda i: (0, 0)
    in_specs, args = [], []
    for a, width in ((os_, Q_WIDTH), (lses, LANES)):
        for arr, d in zip(a, dils):
            in_specs.append(pl.BlockSpec((1, d, tm // d, width), idx))
            args.append(arr.reshape(batch, d, seq // d, width))
    in_specs += [pl.BlockSpec((PERM_ROWS, PERM_ROWS), const)] * len(dils)
    in_specs.append(pl.BlockSpec((2 * LANES, Q_WIDTH), const))
    return pl.pallas_call(
        functools.partial(_mix_kernel, dils=tuple(dils)),
        grid=(batch * n_seq_blocks,),
        in_specs=in_specs,
        out_specs=pl.BlockSpec((tm, Q_WIDTH), lambda i: (i, 0)),
        out_shape=jax.ShapeDtypeStruct((batch * seq, Q_WIDTH), BF16),
        compiler_params=pltpu.CompilerParams(
            dimension_semantics=("arbitrary",), vmem_limit_bytes=VMEM_LIMIT),
        name="group_mix",
    )(*args, *perms, spread)


def _post_kernel(*refs, final, ts):
    x_ref, o_ref, wo_ref, g_ref, wg_ref, wu_ref, wd_ref = refs[:7]
    gf_ref = refs[7] if final else None
    out_ref = refs[-1]
    tm = x_ref.shape[0]
    for sub in range(tm // ts):
        rows = slice(sub * ts, (sub + 1) * ts)
        x1 = x_ref[rows, :] + jnp.dot(o_ref[rows, :], wo_ref[...], preferred_element_type=F32)
        h = _rmsnorm(x1, g_ref[...]).astype(BF16)
        gate = jnp.dot(h, wg_ref[...], preferred_element_type=F32)
        up = jnp.dot(h, wu_ref[...], preferred_element_type=F32)
        act = (gate * jax.nn.sigmoid(gate) * up).astype(BF16)
        x2 = x1 + jnp.dot(act, wd_ref[...], preferred_element_type=F32)
        if final:
            x2 = _rmsnorm(x2, gf_ref[...])
        out_ref[rows, :] = x2


def _post(x2d, o2d, wo, g, wg, wu, wd, gf=None, tm=ROW_TILE, ts=FFN_SUB_TILE):
    t = x2d.shape[0]
    final = gf is not None
    d_ff = wg.shape[1]
    assert t % tm == 0 and tm % ts == 0 and o2d.shape == (t, Q_WIDTH)
    assert wo.shape == (Q_WIDTH, D_MODEL) and wu.shape == wg.shape and wd.shape == (d_ff, D_MODEL)
    tile = lambda i: (i, 0)
    const = lambda i: (0, 0)
    resident = functools.partial(pl.BlockSpec, index_map=const, pipeline_mode=pl.Buffered(1))
    in_specs = [pl.BlockSpec((tm, D_MODEL), tile), pl.BlockSpec((tm, Q_WIDTH), tile),
                resident((Q_WIDTH, D_MODEL)), resident((1, D_MODEL)),
                resident((D_MODEL, d_ff)), resident((D_MODEL, d_ff)), resident((d_ff, D_MODEL))]
    args = [x2d, o2d, wo, g, wg, wu, wd]
    if final:
        in_specs.append(resident((1, D_MODEL)))
        args.append(gf)
    return pl.pallas_call(
        functools.partial(_post_kernel, final=final, ts=ts),
        grid=(t // tm,),
        in_specs=in_specs,
        out_specs=pl.BlockSpec((tm, D_MODEL), tile),
        out_shape=jax.ShapeDtypeStruct((t, D_MODEL), F32),
        compiler_params=pltpu.CompilerParams(
            dimension_semantics=("arbitrary",), vmem_limit_bytes=VMEM_LIMIT),
        name="out_ffn_final" if final else "out_ffn",
    )(*args)


def _rope_tables(seq):
    f32 = np.float32
    inv_freq = (f32(1.0) / (f32(ROPE_THETA) ** (np.arange(0, HEAD_DIM, 2, dtype=f32) / f32(HEAD_DIM))))
    ang = np.arange(seq, dtype=f32)[:, None] * inv_freq[None, :].astype(f32)
    cos = np.cos(ang).astype(f32)
    sin = np.sin(ang).astype(f32)
    cos_t = np.tile(cos, (1, LANES // (HEAD_DIM // 2)))
    sin_t = np.tile(np.concatenate([-sin, sin], axis=1), (1, LANES // HEAD_DIM))
    return cos_t, sin_t


def kernel(x, a_w_in, a_sink, a_w_out, b_w_in, b_w_out, norm_mix, norm_ffn,
           w_gate, w_up, w_down, final_norm):
    b, s, dm = x.shape
    t = b * s
    cos_t, sin_t = _rope_tables(s)
    row = lambda v: v.reshape(1, -1).astype(F32)
    bf = lambda a: a.astype(BF16)

    x2d = x.reshape(t, dm)
    (qkv,) = _proj(x2d, row(norm_mix[0]), bf(a_w_in[0]), cos_t, sin_t, b, s, (1,))
    sink_rows = jnp.repeat(a_sink[0].astype(F32) * LOG2E, Q_BLOCK).reshape(1, N_HEADS * Q_BLOCK)
    o = _attn(qkv, 1, ATTN_HALF_WINDOW, sink_rows=sink_rows)
    x2d = _post(x2d, o.reshape(t, Q_WIDTH), bf(a_w_out[0]), row(norm_ffn[0]),
                bf(w_gate[0]), bf(w_up[0]), bf(w_down[0]))

    dils = tuple(dil for _, dil in DILATED_GROUPS)
    qkvs = _proj(x2d, row(norm_mix[1]), bf(b_w_in[0]), cos_t, sin_t, b, s, dils)
    os_, lses = [], []
    for qkv, (window, dil) in zip(qkvs, DILATED_GROUPS):
        o, lse = _attn(qkv, dil, window // 2 // dil)
        os_.append(o)
        lses.append(lse)
    o = _mix(os_, lses, dils, b, s)
    out = _post(x2d, o, bf(b_w_out[0]), row(norm_ffn[1]),
                bf(w_gate[1]), bf(w_up[1]), bf(w_down[1]), gf=row(final_norm))
    return out.reshape(b, s, dm)
```

```python
import functools
import math

import jax
import jax.numpy as jnp
import numpy as np
from jax import lax
from jax.experimental import pallas as pl
from jax.experimental.pallas import tpu as pltpu

D_MODEL = 1024
HEAD_DIM = 64
N_HEADS = 16
N_KV_HEADS = 4
GQA = N_HEADS // N_KV_HEADS
Q_WIDTH = N_HEADS * HEAD_DIM
KV_WIDTH = N_KV_HEADS * HEAD_DIM
QKV_WIDTH = Q_WIDTH + 2 * KV_WIDTH
ATTN_HALF_WINDOW = 128
DILATED_GROUPS = ((128, 1), (512, 4), (2048, 16))
ROPE_THETA = 10000.0
RMS_EPS = 1e-6
NEG_INF = -1e30
LANES = 128
BF16_SUBLANE_ROWS = 16
VMEM_LIMIT = 56 * 1024 * 1024
ROW_TILE = 1024
FFN_SUB_TILE = 512
PROJ_SUB_TILE = 256

LOG2E = math.log2(math.e)
Q_SCALE = LOG2E / math.sqrt(HEAD_DIM)

BF16 = jnp.bfloat16
F32 = jnp.float32


def _rmsnorm(x, g):
    ms = jnp.mean(x * x, axis=-1, keepdims=True)
    return x * lax.rsqrt(ms + RMS_EPS) * g


def _proj_kernel(*refs, dils, ts):
    n = len(dils)
    x_ref, g_ref, w_ref, cos_ref, sin_ref = refs[:5]
    o_refs = refs[5:5 + n]
    tm = x_ref.shape[0]
    lane = lax.broadcasted_iota(jnp.int32, (ts, LANES), 1)
    first_half = (lane % HEAD_DIM) < (HEAD_DIM // 2)
    n_rope = (Q_WIDTH + KV_WIDTH) // LANES
    n_q = Q_WIDTH // LANES
    for sub in range(tm // ts):
        rows = slice(sub * ts, (sub + 1) * ts)
        h = _rmsnorm(x_ref[rows, :], g_ref[...])
        for gi, (d, o_ref) in enumerate(zip(dils, o_refs)):
            hg = h
            if d > 1:
                hg = jnp.swapaxes(h.reshape(ts // d, d, D_MODEL), 0, 1).reshape(ts, D_MODEL)
            y = jnp.dot(hg.astype(BF16), w_ref[:, gi * QKV_WIDTH:(gi + 1) * QKV_WIDTH],
                        preferred_element_type=F32)
            cos = cos_ref[gi, rows, :]
            sin = sin_ref[gi, rows, :]
            for c in range(QKV_WIDTH // LANES):
                t = y[:, c * LANES:(c + 1) * LANES]
                if c < n_rope:
                    swapped = jnp.where(first_half,
                                        pltpu.roll(t, LANES - HEAD_DIM // 2, 1),
                                        pltpu.roll(t, HEAD_DIM // 2, 1))
                    t = t * cos + swapped * sin
                    if c < n_q:
                        t = t * Q_SCALE
                o_ref[0, :, sub * (ts // d):(sub + 1) * (ts // d), c * LANES:(c + 1) * LANES] = (
                    t.astype(BF16).reshape(d, ts // d, LANES))


def _tile_residue_order(table, d, tm):
    s, n = table.shape
    return table.reshape(s // tm, tm // d, d, n).swapaxes(1, 2).reshape(s, n)


def _proj(x2d, g, w, cos_t, sin_t, batch, seq, dils, tm=ROW_TILE, ts=PROJ_SUB_TILE):
    n = len(dils)
    n_seq_blocks = seq // tm
    assert x2d.shape == (batch * seq, D_MODEL) and w.shape == (D_MODEL, n * QKV_WIDTH)
    assert seq % tm == 0 and tm % ts == 0
    assert all(ts % d == 0 and (ts // d) % BF16_SUBLANE_ROWS == 0 for d in dils)
    cos_g = np.stack([_tile_residue_order(cos_t, d, ts) for d in dils])
    sin_g = np.stack([_tile_residue_order(sin_t, d, ts) for d in dils])
    table_spec = pl.BlockSpec((n, tm, LANES), lambda i: (0, i % n_seq_blocks, 0))
    outs = pl.pallas_call(
        functools.partial(_proj_kernel, dils=tuple(dils), ts=ts),
        grid=(batch * n_seq_blocks,),
        in_specs=[
            pl.BlockSpec((tm, D_MODEL), lambda i: (i, 0)),
            pl.BlockSpec((1, D_MODEL), lambda i: (0, 0)),
            pl.BlockSpec((D_MODEL, n * QKV_WIDTH), lambda i: (0, 0), pipeline_mode=pl.Buffered(1)),
            table_spec, table_spec,
        ],
        out_specs=[pl.BlockSpec((1, d, tm // d, QKV_WIDTH),
                                lambda i: (i // n_seq_blocks, 0, i % n_seq_blocks, 0)) for d in dils],
        out_shape=[jax.ShapeDtypeStruct((batch, d, seq // d, QKV_WIDTH), BF16) for d in dils],
        compiler_params=pltpu.CompilerParams(
            dimension_semantics=("arbitrary",), vmem_limit_bytes=VMEM_LIMIT),
        name=f"qkv_proj_x{n}",
    )(x2d, g, w, cos_g, sin_g)
    return [o.reshape(batch, seq, QKV_WIDTH) for o in outs]


Q_BLOCK = 128
MAX_UNITS_PER_TRIP = 8


def _unit_key_start(i, qb, w, length, nk):
    return min(max(i * qb - w, 0), length - nk)


def _mask_tables(length, qb, w, nk):
    nqb = length // qb
    offs = sorted({_unit_key_start(i, qb, w, length, nk) - i * qb for i in range(nqb)})
    kpos = np.arange(nk)[:, None]
    qpos = np.arange(qb)[None, :]
    tabs = [(np.abs(kpos + off - qpos) <= w).astype(np.float32) for off in offs]
    return np.stack(tabs), offs


def _attn_kernel(*refs, seq, d, w, offs, units_per_trip, has_sink):
    if has_sink:
        sink_ref, mask_ref, qkv_ref, o_ref, s_scr, m_scr = refs
        lse_ref = None
    else:
        mask_ref, qkv_ref, o_ref, lse_ref, s_scr, m_scr = refs
    qb = Q_BLOCK
    length = seq // d
    nqb = length // qb
    nk = min(qb + 2 * w, length)
    n_units = d * nqb
    nl = GQA * qb
    ones_rows = jnp.ones((BF16_SUBLANE_ROWS, nk), BF16)

    def unit_rows(t):
        r = t // nqb
        q0 = (t % nqb) * qb
        ks = jnp.clip(q0 - w, 0, length - nk)
        row0 = pl.multiple_of(r * length + q0, qb)
        krow0 = pl.multiple_of(r * length + ks, BF16_SUBLANE_ROWS)
        return row0, krow0, ks - q0

    def scores(row0, krow0, valid4, g):
        k = qkv_ref[0, pl.ds(krow0, nk), Q_WIDTH + g * HEAD_DIM:Q_WIDTH + (g + 1) * HEAD_DIM]
        q4 = jnp.concatenate(
            [qkv_ref[0, pl.ds(row0, qb), (GQA * g + hh) * HEAD_DIM:(GQA * g + hh + 1) * HEAD_DIM]
             for hh in range(GQA)], axis=0)
        s = lax.dot_general(k, q4, (((1,), (1,)), ((), ())), preferred_element_type=F32)
        s = jnp.where(valid4, s, NEG_INF)
        m = jnp.max(s, axis=0, keepdims=True)
        if has_sink:
            m = jnp.maximum(m, sink_ref[:, g * nl:(g + 1) * nl])
        return s, m

    def finish(row0, krow0, g, s, m):
        p = jnp.exp2(s - m).astype(BF16)
        v = qkv_ref[0, pl.ds(krow0, nk),
                    Q_WIDTH + KV_WIDTH + g * HEAD_DIM:Q_WIDTH + KV_WIDTH + (g + 1) * HEAD_DIM]
        lhs = jnp.concatenate([v.T, ones_rows], axis=0)
        res = jnp.dot(lhs, p, preferred_element_type=F32)
        l = res[HEAD_DIM:HEAD_DIM + 1]
        if has_sink:
            l = l + jnp.exp2(sink_ref[:, g * nl:(g + 1) * nl] - m)
        o_t = res[:HEAD_DIM] * (1.0 / l)
        for hp in range(GQA // 2):
            pair = jnp.concatenate([o_t[:, (2 * hp) * qb:(2 * hp + 1) * qb],
                                    o_t[:, (2 * hp + 1) * qb:(2 * hp + 2) * qb]], axis=0)
            c0 = (GQA * g + 2 * hp) * HEAD_DIM
            o_ref[0, pl.ds(row0, qb), c0:c0 + 2 * HEAD_DIM] = pair.T.astype(BF16)
        if lse_ref is None:
            return []
        lse = (m + jnp.log2(l)) * (1.0 / LOG2E)
        return [lse[:, hh * qb:(hh + 1) * qb] for hh in range(GQA)]

    def unit_mask(off):
        table = sum((off > o).astype(jnp.int32) for o in offs[:-1]) if len(offs) > 1 else 0
        valid = mask_ref[table] > 0.5
        return jnp.concatenate([valid] * GQA, axis=1)

    def step(t_cur, t_next, slot_cur, slot_next):
        nrow0, nkrow0, noff = unit_rows(t_next)
        nvalid4 = unit_mask(noff)
        if t_cur is None:
            for g in range(N_KV_HEADS):
                s, m = scores(nrow0, nkrow0, nvalid4, g)
                s_scr[slot_next, g] = s
                m_scr[slot_next, g] = m
            return
        row0, krow0, _ = unit_rows(t_cur)
        lse_rows = []
        for g in range(N_KV_HEADS):
            s, m = scores(nrow0, nkrow0, nvalid4, g)
            s_scr[slot_next, g] = s
            m_scr[slot_next, g] = m
            lse_rows += finish(row0, krow0, g, s_scr[slot_cur, g], m_scr[slot_cur, g])
        if lse_ref is not None:
            pad = jnp.zeros((LANES - N_HEADS, qb), F32)
            lse_ref[0, pl.ds(row0, qb), :] = jnp.concatenate(lse_rows + [pad], axis=0).T

    step(None, 0, None, 0)

    def body(tt, carry):
        t = units_per_trip * tt
        for u in range(units_per_trip):
            step(t + u, jnp.minimum(t + u + 1, n_units - 1), u % 2, (u + 1) % 2)
        return carry

    lax.fori_loop(0, n_units // units_per_trip, body, 0)


def _attn(qkv, d, w, sink_rows=None):
    b, seq, _ = qkv.shape
    has_sink = sink_rows is not None
    length = seq // d
    nk = min(Q_BLOCK + 2 * w, length)
    masks, offs = _mask_tables(length, Q_BLOCK, w, nk)
    units_per_trip = MAX_UNITS_PER_TRIP if length > Q_BLOCK else MAX_UNITS_PER_TRIP // 2
    assert qkv.shape[2] == QKV_WIDTH and seq % d == 0 and length % Q_BLOCK == 0
    assert w % BF16_SUBLANE_ROWS == 0 and (length - nk) % BF16_SUBLANE_ROWS == 0
    assert units_per_trip % 2 == 0 and (seq // Q_BLOCK) % units_per_trip == 0
    kern = functools.partial(_attn_kernel, seq=seq, d=d, w=w, offs=tuple(offs),
                             units_per_trip=units_per_trip, has_sink=has_sink)
    in_specs = [pl.BlockSpec(masks.shape, lambda i: (0, 0, 0)),
                pl.BlockSpec((1, seq, QKV_WIDTH), lambda i: (i, 0, 0))]
    args = [masks, qkv]
    o_spec = pl.BlockSpec((1, seq, Q_WIDTH), lambda i: (i, 0, 0))
    o_shape = jax.ShapeDtypeStruct((b, seq, Q_WIDTH), BF16)
    if has_sink:
        in_specs = [pl.BlockSpec((1, N_HEADS * Q_BLOCK), lambda i: (0, 0))] + in_specs
        args = [sink_rows] + args
        out_specs, out_shape = o_spec, o_shape
    else:
        out_specs = (o_spec, pl.BlockSpec((1, seq, LANES), lambda i: (i, 0, 0)))
        out_shape = (o_shape, jax.ShapeDtypeStruct((b, seq, LANES), F32))
    return pl.pallas_call(
        kern,
        grid=(b,),
        in_specs=in_specs,
        out_specs=out_specs,
        out_shape=out_shape,
        scratch_shapes=[pltpu.VMEM((2, N_KV_HEADS, nk, GQA * Q_BLOCK), F32),
                        pltpu.VMEM((2, N_KV_HEADS, 1, GQA * Q_BLOCK), F32)],
        compiler_params=pltpu.CompilerParams(
            dimension_semantics=("arbitrary",), vmem_limit_bytes=VMEM_LIMIT),
        name=f"band_attn_d{d}",
    )(*args)


PERM_ROWS = 256


def _mix_kernel(*refs, dils):
    n = len(dils)
    o_refs, lse_refs = refs[:n], refs[n:2 * n]
    perm_refs = refs[2 * n:3 * n]
    e_ref, out_ref = refs[3 * n:]
    tm = out_ref.shape[0]

    def token_order(ref, d):
        a = ref[0]
        if d > 1:
            a = jnp.swapaxes(a, 0, 1)
        return a.reshape(tm, a.shape[-1])

    lses = [token_order(r, d) for r, d in zip(lse_refs, dils)]
    mx = functools.reduce(jnp.maximum, lses)
    es = [jnp.exp(l - mx) for l in lses]
    inv = 1.0 / functools.reduce(jnp.add, es)
    wfulls = []
    for e in es[1:]:
        wt = e * inv
        hi = wt.astype(BF16)
        lo = (wt - hi.astype(F32)).astype(BF16)
        wfulls.append(jnp.dot(jnp.concatenate([hi, lo], axis=1), e_ref[...],
                              preferred_element_type=F32))
    for j in range(tm // PERM_ROWS):
        rows = slice(j * PERM_ROWS, (j + 1) * PERM_ROWS)
        toks = []
        for o_ref, p_ref, d in zip(o_refs, perm_refs, dils):
            blk = o_ref[0, :, j * (PERM_ROWS // d):(j + 1) * (PERM_ROWS // d), :]
            blk = blk.reshape(PERM_ROWS, Q_WIDTH)
            if d > 1:
                toks.append(jnp.dot(p_ref[...], blk, preferred_element_type=F32))
            else:
                toks.append(blk.astype(F32))
        acc = toks[0]
        for wfull, tok in zip(wfulls, toks[1:]):
            acc = acc + wfull[rows, :] * (tok - toks[0])
        out_ref[rows, :] = acc.astype(BF16)


def _mix(os_, lses, dils, batch, seq, tm=ROW_TILE):
    n_seq_blocks = seq // tm
    assert seq % tm == 0 and tm % PERM_ROWS == 0
    assert all(PERM_ROWS % d == 0 and (PERM_ROWS // d) % BF16_SUBLANE_ROWS == 0 for d in dils)
    assert len(os_) == len(lses) == len(dils)
    head_of_lane = np.arange(Q_WIDTH) // HEAD_DIM
    spread = (np.arange(LANES)[:, None] == head_of_lane[None, :]).astype(np.float32)
    spread = jnp.asarray(np.concatenate([spread, spread], axis=0), BF16)
    perms = []
    for d in dils:
        n_tok = np.arange(PERM_ROWS)
        src_row = (n_tok % d) * (PERM_ROWS // d) + n_tok // d
        perms.append(jnp.asarray(np.arange(PERM_ROWS)[None, :] == src_row[:, None], BF16))
    idx = lambda i: (i // n_seq_blocks, 0, i % n_seq_blocks, 0)
    const = lambda i: (0, 0)
    in_specs, args = [], []
    for a, width in ((os_, Q_WIDTH), (lses, LANES)):
        for arr, d in zip(a, dils):
            in_specs.append(pl.BlockSpec((1, d, tm // d, width), idx))
            args.append(arr.reshape(batch, d, seq // d, width))
    in_specs += [pl.BlockSpec((PERM_ROWS, PERM_ROWS), const)] * len(dils)
    in_specs.append(pl.BlockSpec((2 * LANES, Q_WIDTH), const))
    return pl.pallas_call(
        functools.partial(_mix_kernel, dils=tuple(dils)),
        grid=(batch * n_seq_blocks,),
        in_specs=in_specs,
        out_specs=pl.BlockSpec((tm, Q_WIDTH), lambda i: (i, 0)),
        out_shape=jax.ShapeDtypeStruct((batch * seq, Q_WIDTH), BF16),
        compiler_params=pltpu.CompilerParams(
            dimension_semantics=("arbitrary",), vmem_limit_bytes=VMEM_LIMIT),
        name="group_mix",
    )(*args, *perms, spread)


CAST_ROWS = 128


def _post_kernel(*refs, final, ts, layer, mixer_layer):
    n_in = 8 if final else 7
    x_ref, o_ref, wo_hbm, g_ref, wg_hbm, wu_hbm, wd_hbm = refs[:7]
    gf_ref = refs[7] if final else None
    out_ref = refs[n_in]
    wo_ref, wg_ref, wu_ref, wd_ref, stage_wide, stage_narrow, sem = refs[n_in + 1:]
    tm = x_ref.shape[0]

    @pl.when(pl.program_id(0) == 0)
    def _():
        def cast_in(w_hbm, idx, w_ref, stage):
            n_chunks = w_hbm.shape[1] // CAST_ROWS

            def copy(c):
                return pltpu.make_async_copy(w_hbm.at[idx, pl.ds(c * CAST_ROWS, CAST_ROWS), :],
                                             stage.at[c % 2], sem.at[c % 2])

            copy(0).start()
            for c in range(n_chunks):
                if c + 1 < n_chunks:
                    copy(c + 1).start()
                copy(c).wait()
                w_ref[c * CAST_ROWS:(c + 1) * CAST_ROWS, :] = stage[c % 2].astype(BF16)

        cast_in(wo_hbm, mixer_layer, wo_ref, stage_narrow)
        cast_in(wg_hbm, layer, wg_ref, stage_wide)
        cast_in(wu_hbm, layer, wu_ref, stage_wide)
        cast_in(wd_hbm, layer, wd_ref, stage_narrow)

    for sub in range(tm // ts):
        rows = slice(sub * ts, (sub + 1) * ts)
        x1 = x_ref[rows, :] + jnp.dot(o_ref[rows, :], wo_ref[...], preferred_element_type=F32)
        h = _rmsnorm(x1, g_ref[...]).astype(BF16)
        gate = jnp.dot(h, wg_ref[...], preferred_element_type=F32)
        up = jnp.dot(h, wu_ref[...], preferred_element_type=F32)
        act = (gate * jax.nn.sigmoid(gate) * up).astype(BF16)
        x2 = x1 + jnp.dot(act, wd_ref[...], preferred_element_type=F32)
        if final:
            x2 = _rmsnorm(x2, gf_ref[...])
        out_ref[rows, :] = x2


def _post(x2d, o2d, wo, g, wg, wu, wd, layer, mixer_layer, gf=None, tm=ROW_TILE, ts=FFN_SUB_TILE):
    t = x2d.shape[0]
    final = gf is not None
    d_ff = wg.shape[2]
    assert t % tm == 0 and tm % ts == 0 and o2d.shape == (t, Q_WIDTH)
    assert wo.shape[1:] == (Q_WIDTH, D_MODEL) and wu.shape == wg.shape and wd.shape[1:] == (d_ff, D_MODEL)
    assert all(w.dtype == F32 and w.shape[1] % CAST_ROWS == 0 for w in (wo, wg, wu, wd))
    assert mixer_layer < wo.shape[0] and layer < wg.shape[0] == wu.shape[0] == wd.shape[0]
    tile = lambda i: (i, 0)
    const = lambda i: (0, 0)
    resident = functools.partial(pl.BlockSpec, index_map=const, pipeline_mode=pl.Buffered(1))
    hbm = pl.BlockSpec(memory_space=pl.ANY)
    in_specs = [pl.BlockSpec((tm, D_MODEL), tile), pl.BlockSpec((tm, Q_WIDTH), tile),
                hbm, resident((1, D_MODEL)), hbm, hbm, hbm]
    args = [x2d, o2d, wo, g, wg, wu, wd]
    if final:
        in_specs.append(resident((1, D_MODEL)))
        args.append(gf)
    return pl.pallas_call(
        functools.partial(_post_kernel, final=final, ts=ts, layer=layer, mixer_layer=mixer_layer),
        grid=(t // tm,),
        in_specs=in_specs,
        out_specs=pl.BlockSpec((tm, D_MODEL), tile),
        out_shape=jax.ShapeDtypeStruct((t, D_MODEL), F32),
        scratch_shapes=[pltpu.VMEM((Q_WIDTH, D_MODEL), BF16), pltpu.VMEM((D_MODEL, d_ff), BF16),
                        pltpu.VMEM((D_MODEL, d_ff), BF16), pltpu.VMEM((d_ff, D_MODEL), BF16),
                        pltpu.VMEM((2, CAST_ROWS, d_ff), F32), pltpu.VMEM((2, CAST_ROWS, D_MODEL), F32),
                        pltpu.SemaphoreType.DMA((2,))],
        compiler_params=pltpu.CompilerParams(
            dimension_semantics=("arbitrary",), vmem_limit_bytes=VMEM_LIMIT),
        name="out_ffn_final" if final else "out_ffn",
    )(*args)


def _rope_tables(seq):
    f32 = np.float32
    inv_freq = (f32(1.0) / (f32(ROPE_THETA) ** (np.arange(0, HEAD_DIM, 2, dtype=f32) / f32(HEAD_DIM))))
    ang = np.arange(seq, dtype=f32)[:, None] * inv_freq[None, :].astype(f32)
    cos = np.cos(ang).astype(f32)
    sin = np.sin(ang).astype(f32)
    cos_t = np.tile(cos, (1, LANES // (HEAD_DIM // 2)))
    sin_t = np.tile(np.concatenate([-sin, sin], axis=1), (1, LANES // HEAD_DIM))
    return cos_t, sin_t


def kernel(x, a_w_in, a_sink, a_w_out, b_w_in, b_w_out, norm_mix, norm_ffn,
           w_gate, w_up, w_down, final_norm):
    b, s, dm = x.shape
    t = b * s
    cos_t, sin_t = _rope_tables(s)
    row = lambda v: v.reshape(1, -1).astype(F32)
    bf = lambda a: a.astype(BF16)

    x2d = x.reshape(t, dm)
    (qkv,) = _proj(x2d, row(norm_mix[0]), bf(a_w_in[0]), cos_t, sin_t, b, s, (1,))
    sink_rows = jnp.repeat(a_sink[0].astype(F32) * LOG2E, Q_BLOCK).reshape(1, N_HEADS * Q_BLOCK)
    o = _attn(qkv, 1, ATTN_HALF_WINDOW, sink_rows=sink_rows)
    x2d = _post(x2d, o.reshape(t, Q_WIDTH), a_w_out, row(norm_ffn[0]), w_gate, w_up, w_down, 0, 0)

    dils = tuple(dil for _, dil in DILATED_GROUPS)
    qkvs = _proj(x2d, row(norm_mix[1]), bf(b_w_in[0]), cos_t, sin_t, b, s, dils)
    os_, lses = [], []
    for qkv, (window, dil) in zip(qkvs, DILATED_GROUPS):
        o, lse = _attn(qkv, dil, window // 2 // dil)
        os_.append(o)
        lses.append(lse)
    o = _mix(os_, lses, dils, b, s)
    out = _post(x2d, o, b_w_out, row(norm_ffn[1]), w_gate, w_up, w_down, 1, 0,
                gf=row(final_norm))
    return out.reshape(b, s, dm)
```

```python
import functools
import math

import jax
import jax.numpy as jnp
import numpy as np
from jax import lax
from jax.experimental import pallas as pl
from jax.experimental.pallas import tpu as pltpu

D_MODEL = 1024
HEAD_DIM = 64
N_HEADS = 16
N_KV_HEADS = 4
GQA = N_HEADS // N_KV_HEADS
Q_WIDTH = N_HEADS * HEAD_DIM
KV_WIDTH = N_KV_HEADS * HEAD_DIM
QKV_WIDTH = Q_WIDTH + 2 * KV_WIDTH
ATTN_HALF_WINDOW = 128
DILATED_GROUPS = ((128, 1), (512, 4), (2048, 16))
ROPE_THETA = 10000.0
RMS_EPS = 1e-6
NEG_INF = -1e30
LANES = 128
BF16_SUBLANE_ROWS = 16
VMEM_LIMIT = 56 * 1024 * 1024
ROW_TILE = 1024
FFN_SUB_TILE = 512
PROJ_SUB_TILE = 256

LOG2E = math.log2(math.e)
Q_SCALE = LOG2E / math.sqrt(HEAD_DIM)

BF16 = jnp.bfloat16
F32 = jnp.float32


def _rmsnorm(x, g):
    ms = jnp.mean(x * x, axis=-1, keepdims=True)
    return x * lax.rsqrt(ms + RMS_EPS) * g


def _proj_kernel(*refs, dils, ts):
    n = len(dils)
    x_ref, g_ref, w_ref, cos_ref, sin_ref = refs[:5]
    o_refs = refs[5:5 + n]
    tm = x_ref.shape[0]
    lane = lax.broadcasted_iota(jnp.int32, (ts, LANES), 1)
    first_half = (lane % HEAD_DIM) < (HEAD_DIM // 2)
    n_rope = (Q_WIDTH + KV_WIDTH) // LANES
    n_q = Q_WIDTH // LANES
    for sub in range(tm // ts):
        rows = slice(sub * ts, (sub + 1) * ts)
        h = _rmsnorm(x_ref[rows, :], g_ref[...])
        for gi, (d, o_ref) in enumerate(zip(dils, o_refs)):
            hg = h
            if d > 1:
                hg = jnp.swapaxes(h.reshape(ts // d, d, D_MODEL), 0, 1).reshape(ts, D_MODEL)
            y = jnp.dot(hg.astype(BF16), w_ref[:, gi * QKV_WIDTH:(gi + 1) * QKV_WIDTH],
                        preferred_element_type=F32)
            cos = cos_ref[gi, rows, :]
            sin = sin_ref[gi, rows, :]
            for c in range(QKV_WIDTH // LANES):
                t = y[:, c * LANES:(c + 1) * LANES]
                if c < n_rope:
                    swapped = jnp.where(first_half,
                                        pltpu.roll(t, LANES - HEAD_DIM // 2, 1),
                                        pltpu.roll(t, HEAD_DIM // 2, 1))
                    t = t * cos + swapped * sin
                    if c < n_q:
                        t = t * Q_SCALE
                o_ref[0, :, sub * (ts // d):(sub + 1) * (ts // d), c * LANES:(c + 1) * LANES] = (
                    t.astype(BF16).reshape(d, ts // d, LANES))


def _tile_residue_order(table, d, tm):
    s, n = table.shape
    return table.reshape(s // tm, tm // d, d, n).swapaxes(1, 2).reshape(s, n)


def _proj(x2d, g, w, cos_t, sin_t, batch, seq, dils, tm=ROW_TILE, ts=PROJ_SUB_TILE):
    n = len(dils)
    n_seq_blocks = seq // tm
    assert x2d.shape == (batch * seq, D_MODEL) and w.shape == (D_MODEL, n * QKV_WIDTH)
    assert seq % tm == 0 and tm % ts == 0
    assert all(ts % d == 0 and (ts // d) % BF16_SUBLANE_ROWS == 0 for d in dils)
    cos_g = np.stack([_tile_residue_order(cos_t, d, ts) for d in dils])
    sin_g = np.stack([_tile_residue_order(sin_t, d, ts) for d in dils])
    table_spec = pl.BlockSpec((n, tm, LANES), lambda i: (0, i % n_seq_blocks, 0))
    outs = pl.pallas_call(
        functools.partial(_proj_kernel, dils=tuple(dils), ts=ts),
        grid=(batch * n_seq_blocks,),
        in_specs=[
            pl.BlockSpec((tm, D_MODEL), lambda i: (i, 0)),
            pl.BlockSpec((1, D_MODEL), lambda i: (0, 0)),
            pl.BlockSpec((D_MODEL, n * QKV_WIDTH), lambda i: (0, 0), pipeline_mode=pl.Buffered(1)),
            table_spec, table_spec,
        ],
        out_specs=[pl.BlockSpec((1, d, tm // d, QKV_WIDTH),
                                lambda i: (i // n_seq_blocks, 0, i % n_seq_blocks, 0)) for d in dils],
        out_shape=[jax.ShapeDtypeStruct((batch, d, seq // d, QKV_WIDTH), BF16) for d in dils],
        compiler_params=pltpu.CompilerParams(
            dimension_semantics=("arbitrary",), vmem_limit_bytes=VMEM_LIMIT),
        name=f"qkv_proj_x{n}",
    )(x2d, g, w, cos_g, sin_g)
    return [o.reshape(batch, seq, QKV_WIDTH) for o in outs]


Q_BLOCK = 128
MAX_UNITS_PER_TRIP = 8


def _unit_key_start(i, qb, w, length, nk):
    return min(max(i * qb - w, 0), length - nk)


def _mask_tables(length, qb, w, nk):
    nqb = length // qb
    offs = sorted({_unit_key_start(i, qb, w, length, nk) - i * qb for i in range(nqb)})
    kpos = np.arange(nk)[:, None]
    qpos = np.arange(qb)[None, :]
    tabs = [np.where(np.abs(kpos + off - qpos) <= w, 0.0, NEG_INF).astype(np.float32) for off in offs]
    return np.stack(tabs), offs


def _attn_kernel(*refs, seq, d, w, offs, units_per_trip, has_sink):
    if has_sink:
        sink_ref, mask_ref, qkv_ref, o_ref, s_scr, m_scr = refs
        lse_ref = None
    else:
        mask_ref, qkv_ref, o_ref, lse_ref, s_scr, m_scr = refs
    qb = Q_BLOCK
    length = seq // d
    nqb = length // qb
    nk = min(qb + 2 * w, length)
    n_units = d * nqb
    nl = GQA * qb
    ones_rows = jnp.ones((BF16_SUBLANE_ROWS, nk), BF16)

    def unit_rows(t):
        r = t // nqb
        q0 = (t % nqb) * qb
        ks = jnp.clip(q0 - w, 0, length - nk)
        row0 = pl.multiple_of(r * length + q0, qb)
        krow0 = pl.multiple_of(r * length + ks, BF16_SUBLANE_ROWS)
        return row0, krow0, ks - q0

    def scores(row0, krow0, valid4, g):
        k = qkv_ref[0, pl.ds(krow0, nk), Q_WIDTH + g * HEAD_DIM:Q_WIDTH + (g + 1) * HEAD_DIM]
        q4 = jnp.concatenate(
            [qkv_ref[0, pl.ds(row0, qb), (GQA * g + hh) * HEAD_DIM:(GQA * g + hh + 1) * HEAD_DIM]
             for hh in range(GQA)], axis=0)
        s = lax.dot_general(k, q4, (((1,), (1,)), ((), ())), preferred_element_type=F32)
        s = s + valid4
        m = jnp.max(s, axis=0, keepdims=True)
        if has_sink:
            m = jnp.maximum(m, sink_ref[:, g * nl:(g + 1) * nl])
        return s, m

    def finish(row0, krow0, g, s, m):
        p = jnp.exp2(s - m).astype(BF16)
        v = qkv_ref[0, pl.ds(krow0, nk),
                    Q_WIDTH + KV_WIDTH + g * HEAD_DIM:Q_WIDTH + KV_WIDTH + (g + 1) * HEAD_DIM]
        lhs = jnp.concatenate([v.T, ones_rows], axis=0)
        res = jnp.dot(lhs, p, preferred_element_type=F32)
        l = res[HEAD_DIM:HEAD_DIM + 1]
        if has_sink:
            l = l + jnp.exp2(sink_ref[:, g * nl:(g + 1) * nl] - m)
        o_t = res[:HEAD_DIM] * (1.0 / l)
        for hp in range(GQA // 2):
            pair = jnp.concatenate([o_t[:, (2 * hp) * qb:(2 * hp + 1) * qb],
                                    o_t[:, (2 * hp + 1) * qb:(2 * hp + 2) * qb]], axis=0)
            c0 = (GQA * g + 2 * hp) * HEAD_DIM
            o_ref[0, pl.ds(row0, qb), c0:c0 + 2 * HEAD_DIM] = pair.T.astype(BF16)
        if lse_ref is None:
            return []
        lse = (m + jnp.log2(l)) * (1.0 / LOG2E)
        return [lse[:, hh * qb:(hh + 1) * qb] for hh in range(GQA)]

    def unit_mask(off):
        table = sum((off > o).astype(jnp.int32) for o in offs[:-1]) if len(offs) > 1 else 0
        valid = mask_ref[table]
        return jnp.concatenate([valid] * GQA, axis=1)

    def step(t_cur, t_next, slot_cur, slot_next):
        nrow0, nkrow0, noff = unit_rows(t_next)
        nvalid4 = unit_mask(noff)
        if t_cur is None:
            for g in range(N_KV_HEADS):
                s, m = scores(nrow0, nkrow0, nvalid4, g)
                s_scr[slot_next, g] = s
                m_scr[slot_next, g] = m
            return
        row0, krow0, _ = unit_rows(t_cur)
        lse_rows = []
        for g in range(N_KV_HEADS):
            s, m = scores(nrow0, nkrow0, nvalid4, g)
            s_scr[slot_next, g] = s
            m_scr[slot_next, g] = m
            lse_rows += finish(row0, krow0, g, s_scr[slot_cur, g], m_scr[slot_cur, g])
        if lse_ref is not None:
            pad = jnp.zeros((LANES - N_HEADS, qb), F32)
            lse_ref[0, pl.ds(row0, qb), :] = jnp.concatenate(lse_rows + [pad], axis=0).T

    step(None, 0, None, 0)

    def body(tt, carry):
        t = units_per_trip * tt
        for u in range(units_per_trip):
            step(t + u, jnp.minimum(t + u + 1, n_units - 1), u % 2, (u + 1) % 2)
        return carry

    lax.fori_loop(0, n_units // units_per_trip, body, 0)


def _attn(qkv, d, w, sink_rows=None):
    b, seq, _ = qkv.shape
    has_sink = sink_rows is not None
    length = seq // d
    nk = min(Q_BLOCK + 2 * w, length)
    masks, offs = _mask_tables(length, Q_BLOCK, w, nk)
    units_per_trip = MAX_UNITS_PER_TRIP if length > Q_BLOCK else MAX_UNITS_PER_TRIP // 2
    assert qkv.shape[2] == QKV_WIDTH and seq % d == 0 and length % Q_BLOCK == 0
    assert w % BF16_SUBLANE_ROWS == 0 and (length - nk) % BF16_SUBLANE_ROWS == 0
    assert units_per_trip % 2 == 0 and (seq // Q_BLOCK) % units_per_trip == 0
    kern = functools.partial(_attn_kernel, seq=seq, d=d, w=w, offs=tuple(offs),
                             units_per_trip=units_per_trip, has_sink=has_sink)
    in_specs = [pl.BlockSpec(masks.shape, lambda i: (0, 0, 0)),
                pl.BlockSpec((1, seq, QKV_WIDTH), lambda i: (i, 0, 0))]
    args = [masks, qkv]
    o_spec = pl.BlockSpec((1, seq, Q_WIDTH), lambda i: (i, 0, 0))
    o_shape = jax.ShapeDtypeStruct((b, seq, Q_WIDTH), BF16)
    if has_sink:
        in_specs = [pl.BlockSpec((1, N_HEADS * Q_BLOCK), lambda i: (0, 0))] + in_specs
        args = [sink_rows] + args
        out_specs, out_shape = o_spec, o_shape
    else:
        out_specs = (o_spec, pl.BlockSpec((1, seq, LANES), lambda i: (i, 0, 0)))
        out_shape = (o_shape, jax.ShapeDtypeStruct((b, seq, LANES), F32))
    return pl.pallas_call(
        kern,
        grid=(b,),
        in_specs=in_specs,
        out_specs=out_specs,
        out_shape=out_shape,
        scratch_shapes=[pltpu.VMEM((2, N_KV_HEADS, nk, GQA * Q_BLOCK), F32),
                        pltpu.VMEM((2, N_KV_HEADS, 1, GQA * Q_BLOCK), F32)],
        compiler_params=pltpu.CompilerParams(
            dimension_semantics=("arbitrary",), vmem_limit_bytes=VMEM_LIMIT),
        name=f"band_attn_d{d}",
    )(*args)


PERM_ROWS = 256


def _mix_kernel(*refs, dils):
    n = len(dils)
    o_refs, lse_refs = refs[:n], refs[n:2 * n]
    perm_refs = refs[2 * n:3 * n]
    e_ref, out_ref = refs[3 * n:]
    tm = out_ref.shape[0]

    def token_order(ref, d):
        a = ref[0]
        if d > 1:
            a = jnp.swapaxes(a, 0, 1)
        return a.reshape(tm, a.shape[-1])

    lses = [token_order(r, d) for r, d in zip(lse_refs, dils)]
    mx = functools.reduce(jnp.maximum, lses)
    es = [jnp.exp(l - mx) for l in lses]
    inv = 1.0 / functools.reduce(jnp.add, es)
    wfulls = []
    for e in es[1:]:
        wt = e * inv
        hi = wt.astype(BF16)
        lo = (wt - hi.astype(F32)).astype(BF16)
        wfulls.append(jnp.dot(jnp.concatenate([hi, lo], axis=1), e_ref[...],
                              preferred_element_type=F32))
    for j in range(tm // PERM_ROWS):
        rows = slice(j * PERM_ROWS, (j + 1) * PERM_ROWS)
        toks = []
        for o_ref, p_ref, d in zip(o_refs, perm_refs, dils):
            blk = o_ref[0, :, j * (PERM_ROWS // d):(j + 1) * (PERM_ROWS // d), :]
            blk = blk.reshape(PERM_ROWS, Q_WIDTH)
            if d > 1:
                toks.append(jnp.dot(p_ref[...], blk, preferred_element_type=F32))
            else:
                toks.append(blk.astype(F32))
        acc = toks[0]
        for wfull, tok in zip(wfulls, toks[1:]):
            acc = acc + wfull[rows, :] * (tok - toks[0])
        out_ref[rows, :] = acc.astype(BF16)


def _mix(os_, lses, dils, batch, seq, tm=ROW_TILE):
    n_seq_blocks = seq // tm
    assert seq % tm == 0 and tm % PERM_ROWS == 0
    assert all(PERM_ROWS % d == 0 and (PERM_ROWS // d) % BF16_SUBLANE_ROWS == 0 for d in dils)
    assert len(os_) == len(lses) == len(dils)
    head_of_lane = np.arange(Q_WIDTH) // HEAD_DIM
    spread = (np.arange(LANES)[:, None] == head_of_lane[None, :]).astype(np.float32)
    spread = jnp.asarray(np.concatenate([spread, spread], axis=0), BF16)
    perms = []
    for d in dils:
        n_tok = np.arange(PERM_ROWS)
        src_row = (n_tok % d) * (PERM_ROWS // d) + n_tok // d
        perms.append(jnp.asarray(np.arange(PERM_ROWS)[None, :] == src_row[:, None], BF16))
    idx = lambda i: (i // n_seq_blocks, 0, i % n_seq_blocks, 0)
    const = lambda i: (0, 0)
    in_specs, args = [], []
    for a, width in ((os_, Q_WIDTH), (lses, LANES)):
        for arr, d in zip(a, dils):
            in_specs.append(pl.BlockSpec((1, d, tm // d, width), idx))
            args.append(arr.reshape(batch, d, seq // d, width))
    in_specs += [pl.BlockSpec((PERM_ROWS, PERM_ROWS), const)] * len(dils)
    in_specs.append(pl.BlockSpec((2 * LANES, Q_WIDTH), const))
    return pl.pallas_call(
        functools.partial(_mix_kernel, dils=tuple(dils)),
        grid=(batch * n_seq_blocks,),
        in_specs=in_specs,
        out_specs=pl.BlockSpec((tm, Q_WIDTH), lambda i: (i, 0)),
        out_shape=jax.ShapeDtypeStruct((batch * seq, Q_WIDTH), BF16),
        compiler_params=pltpu.CompilerParams(
            dimension_semantics=("arbitrary",), vmem_limit_bytes=VMEM_LIMIT),
        name="group_mix",
    )(*args, *perms, spread)


def _post_kernel(*refs, final, ts):
    x_ref, o_ref, wo_ref, g_ref, wg_ref, wu_ref, wd_ref = refs[:7]
    gf_ref = refs[7] if final else None
    out_ref = refs[-1]
    tm = x_ref.shape[0]
    for sub in range(tm // ts):
        rows = slice(sub * ts, (sub + 1) * ts)
        x1 = x_ref[rows, :] + jnp.dot(o_ref[rows, :], wo_ref[...], preferred_element_type=F32)
        h = _rmsnorm(x1, g_ref[...]).astype(BF16)
        gate = jnp.dot(h, wg_ref[...], preferred_element_type=F32)
        up = jnp.dot(h, wu_ref[...], preferred_element_type=F32)
        act = (gate * jax.nn.sigmoid(gate) * up).astype(BF16)
        x2 = x1 + jnp.dot(act, wd_ref[...], preferred_element_type=F32)
        if final:
            x2 = _rmsnorm(x2, gf_ref[...])
        out_ref[rows, :] = x2


def _post(x2d, o2d, wo, g, wg, wu, wd, gf=None, tm=ROW_TILE, ts=FFN_SUB_TILE):
    t = x2d.shape[0]
    final = gf is not None
    d_ff = wg.shape[1]
    assert t % tm == 0 and tm % ts == 0 and o2d.shape == (t, Q_WIDTH)
    assert wo.shape == (Q_WIDTH, D_MODEL) and wu.shape == wg.shape and wd.shape == (d_ff, D_MODEL)
    tile = lambda i: (i, 0)
    const = lambda i: (0, 0)
    resident = functools.partial(pl.BlockSpec, index_map=const, pipeline_mode=pl.Buffered(1))
    in_specs = [pl.BlockSpec((tm, D_MODEL), tile), pl.BlockSpec((tm, Q_WIDTH), tile),
                resident((Q_WIDTH, D_MODEL)), resident((1, D_MODEL)),
                resident((D_MODEL, d_ff)), resident((D_MODEL, d_ff)), resident((d_ff, D_MODEL))]
    args = [x2d, o2d, wo, g, wg, wu, wd]
    if final:
        in_specs.append(resident((1, D_MODEL)))
        args.append(gf)
    return pl.pallas_call(
        functools.partial(_post_kernel, final=final, ts=ts),
        grid=(t // tm,),
        in_specs=in_specs,
        out_specs=pl.BlockSpec((tm, D_MODEL), tile),
        out_shape=jax.ShapeDtypeStruct((t, D_MODEL), F32),
        compiler_params=pltpu.CompilerParams(
            dimension_semantics=("arbitrary",), vmem_limit_bytes=VMEM_LIMIT),
        name="out_ffn_final" if final else "out_ffn",
    )(*args)


def _rope_tables(seq):
    f32 = np.float32
    inv_freq = (f32(1.0) / (f32(ROPE_THETA) ** (np.arange(0, HEAD_DIM, 2, dtype=f32) / f32(HEAD_DIM))))
    ang = np.arange(seq, dtype=f32)[:, None] * inv_freq[None, :].astype(f32)
    cos = np.cos(ang).astype(f32)
    sin = np.sin(ang).astype(f32)
    cos_t = np.tile(cos, (1, LANES // (HEAD_DIM // 2)))
    sin_t = np.tile(np.concatenate([-sin, sin], axis=1), (1, LANES // HEAD_DIM))
    return cos_t, sin_t


def kernel(x, a_w_in, a_sink, a_w_out, b_w_in, b_w_out, norm_mix, norm_ffn,
           w_gate, w_up, w_down, final_norm):
    b, s, dm = x.shape
    t = b * s
    cos_t, sin_t = _rope_tables(s)
    row = lambda v: v.reshape(1, -1).astype(F32)
    bf = lambda a: a.astype(BF16)

    x2d = x.reshape(t, dm)
    (qkv,) = _proj(x2d, row(norm_mix[0]), bf(a_w_in[0]), cos_t, sin_t, b, s, (1,))
    sink_rows = jnp.repeat(a_sink[0].astype(F32) * LOG2E, Q_BLOCK).reshape(1, N_HEADS * Q_BLOCK)
    o = _attn(qkv, 1, ATTN_HALF_WINDOW, sink_rows=sink_rows)
    x2d = _post(x2d, o.reshape(t, Q_WIDTH), bf(a_w_out[0]), row(norm_ffn[0]),
                bf(w_gate[0]), bf(w_up[0]), bf(w_down[0]))

    dils = tuple(dil for _, dil in DILATED_GROUPS)
    qkvs = _proj(x2d, row(norm_mix[1]), bf(b_w_in[0]), cos_t, sin_t, b, s, dils)
    os_, lses = [], []
    for qkv, (window, dil) in zip(qkvs, DILATED_GROUPS):
        o, lse = _attn(qkv, dil, window // 2 // dil)
        os_.append(o)
        lses.append(lse)
    o = _mix(os_, lses, dils, b, s)
    out = _post(x2d, o, bf(b_w_out[0]), row(norm_ffn[1]),
                bf(w_gate[1]), bf(w_up[1]), bf(w_down[1]), gf=row(final_norm))
    return out.reshape(b, s, dm)
```
